```python
import jax, jax.numpy as jnp
from jax import lax
import numpy as np

D_MODEL = 4096
BATCH = 1
SEQ = 8192
DEPTH = 4

MIX_WIDTH = D_MODEL
HEAD_DIM = 128
N_HEADS_TOTAL = MIX_WIDTH // HEAD_DIM
B_HEADS = N_HEADS_TOTAL // 4
A_HEADS = (N_HEADS_TOTAL - B_HEADS) // 2
C_HEADS = N_HEADS_TOTAL - A_HEADS - B_HEADS
A_W = A_HEADS * HEAD_DIM
B_W = B_HEADS * HEAD_DIM
C_W = C_HEADS * HEAD_DIM
IN_COLS = 3 * A_W + 2 * B_W + 2 * C_W
SPLIT_POINTS = (A_W, 2 * A_W, 3 * A_W, 3 * A_W + B_W, 3 * A_W + 2 * B_W, 3 * A_W + 2 * B_W + C_W)
SHORT_CONV_K = 3
CFM_CONV_K = 31
CHUNK = 128
N_GROUPS = 4
EXPERTS_PER_GROUP = 8
N_EXPERTS = N_GROUPS * EXPERTS_PER_GROUP
EXPERT_FF = 256
TOP_K = 2
PLE_DIM = 256
EPS = 1e-6

kernel_name = "hybrid_conv_sgu_conformer_hmoe"


def rms_norm(x, g):
    x32 = x.astype(jnp.float32)
    y = x32 * lax.rsqrt(jnp.mean(x32 * x32, axis=-1, keepdims=True) + EPS)
    return y.astype(x.dtype) * g


def layer_norm(x, g, b):
    x32 = x.astype(jnp.float32)
    mu = jnp.mean(x32, axis=-1, keepdims=True)
    var = jnp.mean(jnp.square(x32 - mu), axis=-1, keepdims=True)
    y = (x32 - mu) * lax.rsqrt(var + EPS)
    return y.astype(x.dtype) * g + b


def causal_depthwise_conv(x, w):
    k = w.shape[0]
    return lax.conv_general_dilated(
        x, w[:, None, :].astype(x.dtype), window_strides=(1,), padding=[(k - 1, 0)],
        dimension_numbers=('NWC', 'WIO', 'NWC'), feature_group_count=x.shape[-1])


def short_gated_conv(a_b, a_c, a_x, conv_w):
    return a_b * causal_depthwise_conv(a_c * a_x, conv_w)


def spatial_gating(u, v, ln_g, ln_b, w_s, b_s):
    zu = jax.nn.gelu(u)
    zv = layer_norm(jax.nn.gelu(v), ln_g, ln_b)
    bsz, t, _ = zv.shape
    vc = zv.reshape(bsz, t // CHUNK, CHUNK, B_HEADS, HEAD_DIM)
    mask = jnp.tril(jnp.ones((CHUNK, CHUNK), dtype=bool))
    ws = jnp.where(mask[None], w_s, jnp.zeros((), w_s.dtype))
    s = jnp.einsum('hij,bnjhd->bnihd', ws, vc) + jnp.transpose(b_s)[None, None, :, :, None]
    return zu * s.reshape(bsz, t, B_W)


def conformer_conv(c_a, c_g, conv_w, conv_b, ln_g, ln_b):
    g = c_a * jax.nn.sigmoid(c_g)
    g = causal_depthwise_conv(g, conv_w) + conv_b
    return jax.nn.silu(layer_norm(g, ln_g, ln_b))


def hierarchical_moe(h, rg_w, rg_b, re_w, re_b, w1, w3, w2):
    bsz, t, d = h.shape
    hf = h.reshape(bsz * t, d)
    g_logits = (hf @ rg_w + rg_b).astype(jnp.float32)
    g_probs = jax.nn.softmax(g_logits, axis=-1)
    g_idx = jnp.argmax(g_logits, axis=-1)
    g_p = jnp.take_along_axis(g_probs, g_idx[:, None], axis=-1)
    e_logits = (hf @ re_w + re_b).astype(jnp.float32).reshape(-1, N_GROUPS, EXPERTS_PER_GROUP)
    e_sel = jnp.take_along_axis(e_logits, g_idx[:, None, None], axis=1)[:, 0]
    top_v, top_i = lax.top_k(e_sel, TOP_K)
    top_w = jax.nn.softmax(top_v, axis=-1) * g_p
    eid = g_idx[:, None] * EXPERTS_PER_GROUP + top_i
    gates = jnp.sum(jax.nn.one_hot(eid, N_EXPERTS, dtype=jnp.float32) * top_w[..., None], axis=1)
    gates = gates.astype(h.dtype)
    hid = jax.nn.silu(jnp.einsum('td,edf->tef', hf, w1)) * jnp.einsum('td,edf->tef', hf, w3)
    y = jnp.einsum('tef,efd->td', hid * gates[:, :, None], w2)
    return y.reshape(bsz, t, d)


def setup_inputs(seed: int = 0) -> dict:
    key = jax.random.key(seed)
    ks = jax.random.split(key, 32)
    f32 = jnp.float32

    def nrm(k, shape, scale):
        return jax.random.normal(k, shape, f32) * scale

    def gain(k, shape):
        return 1.0 + 0.05 * jax.random.normal(k, shape, f32)

    L, D = DEPTH, D_MODEL
    return {
        "x": nrm(ks[0], (BATCH, SEQ, D), 1.0),
        "p": nrm(ks[1], (DEPTH, BATCH, SEQ, PLE_DIM), 1.0),
        "norm_mix": gain(ks[2], (L, D)),
        "w_in": nrm(ks[3], (L, D, IN_COLS), D ** -0.5),
        "conv3_w": nrm(ks[4], (L, SHORT_CONV_K, A_W), SHORT_CONV_K ** -0.5),
        "sgu_ln_g": gain(ks[5], (L, B_W)),
        "sgu_ln_b": nrm(ks[6], (L, B_W), 0.02),
        "sgu_w": nrm(ks[7], (L, B_HEADS, CHUNK, CHUNK), CHUNK ** -0.5),
        "sgu_b": gain(ks[8], (L, B_HEADS, CHUNK)),
        "cfm_conv_w": nrm(ks[9], (L, CFM_CONV_K, C_W), CFM_CONV_K ** -0.5),
        "cfm_conv_b": nrm(ks[10], (L, C_W), 0.02),
        "cfm_ln_g": gain(ks[11], (L, C_W)),
        "cfm_ln_b": nrm(ks[12], (L, C_W), 0.02),
        "w_out": nrm(ks[13], (L, MIX_WIDTH, D), MIX_WIDTH ** -0.5),
        "norm_moe": gain(ks[14], (L, D)),
        "router_group_w": nrm(ks[15], (L, D, N_GROUPS), D ** -0.5),
        "router_group_b": nrm(ks[16], (L, N_GROUPS), 0.01),
        "router_expert_w": nrm(ks[17], (L, D, N_EXPERTS), D ** -0.5),
        "router_expert_b": nrm(ks[18], (L, N_EXPERTS), 0.01),
        "exp_w1": nrm(ks[19], (L, N_EXPERTS, D, EXPERT_FF), D ** -0.5),
        "exp_w3": nrm(ks[20], (L, N_EXPERTS, D, EXPERT_FF), D ** -0.5),
        "exp_w2": nrm(ks[21], (L, N_EXPERTS, EXPERT_FF, D), EXPERT_FF ** -0.5),
        "norm_ple": gain(ks[22], (L, D)),
        "ple_gate_w": nrm(ks[23], (L, D, D), D ** -0.5),
        "ple_proj_w": nrm(ks[24], (L, PLE_DIM, D), PLE_DIM ** -0.5),
        "final_norm": gain(ks[25], (D,)),
    }


def reference(x, p, norm_mix, w_in, conv3_w, sgu_ln_g, sgu_ln_b, sgu_w, sgu_b,
              cfm_conv_w, cfm_conv_b, cfm_ln_g, cfm_ln_b, w_out, norm_moe,
              router_group_w, router_group_b, router_expert_w, router_expert_b,
              exp_w1, exp_w3, exp_w2, norm_ple, ple_gate_w, ple_proj_w, final_norm):
    h = x
    for i in range(DEPTH):
        hn = rms_norm(h, norm_mix[i])
        proj = jnp.einsum('btd,dk->btk', hn, w_in[i])
        a_b, a_c, a_x, b_u, b_v, c_a, c_g = jnp.split(proj, SPLIT_POINTS, axis=-1)
        y_a = short_gated_conv(a_b, a_c, a_x, conv3_w[i])
        y_b = spatial_gating(b_u, b_v, sgu_ln_g[i], sgu_ln_b[i], sgu_w[i], sgu_b[i])
        y_c = conformer_conv(c_a, c_g, cfm_conv_w[i], cfm_conv_b[i], cfm_ln_g[i], cfm_ln_b[i])
        y = jnp.concatenate([y_a, y_b, y_c], axis=-1)
        h = h + jnp.einsum('btk,kd->btd', y, w_out[i])
        hn = rms_norm(h, norm_moe[i])
        h = h + hierarchical_moe(hn, router_group_w[i], router_group_b[i],
                                 router_expert_w[i], router_expert_b[i],
                                 exp_w1[i], exp_w3[i], exp_w2[i])
        gate = jax.nn.sigmoid(jnp.einsum('btd,de->bte', rms_norm(h, norm_ple[i]), ple_gate_w[i]))
        h = h + gate * jnp.einsum('btr,rd->btd', p[i], ple_proj_w[i])
    return rms_norm(h, final_norm)
```

```python
import functools

import jax
import jax.numpy as jnp
from jax import lax
from jax.experimental import pallas as pl
from jax.experimental.pallas import tpu as pltpu

F32 = jnp.float32
BF16 = jnp.bfloat16
EPS = 1e-6

HEAD_DIM = 128
CHUNK = 128
N_B_HEADS = 8
A_W = 12 * HEAD_DIM
B_W = N_B_HEADS * HEAD_DIM
C_W = 12 * HEAD_DIM
MIX_W = A_W + B_W + C_W
OFF_AB, OFF_AC, OFF_AX = 0, A_W, 2 * A_W
OFF_BU, OFF_BV = 3 * A_W, 3 * A_W + B_W
OFF_CA, OFF_CG = 3 * A_W + 2 * B_W, 3 * A_W + 2 * B_W + C_W
IN_COLS = 3 * A_W + 2 * B_W + 2 * C_W
SHORT_K = 3
CFM_K = 31
N_GROUPS = 4
EXPERTS_PER_GROUP = 8
N_EXPERTS = N_GROUPS * EXPERTS_PER_GROUP
ROUTER_LANES = 128
SUBLANES = 8
CFM_HALO = 32
VMEM_LIMIT = 56 * 1024 * 1024


def _params(n_axes=1):
    return pltpu.CompilerParams(dimension_semantics=("arbitrary",) * n_axes,
                                vmem_limit_bytes=VMEM_LIMIT)


def _rms_kernel(x_ref, g_ref, o_ref):
    x = x_ref[...]
    ms = jnp.mean(x * x, axis=-1, keepdims=True)
    o_ref[...] = (x * lax.rsqrt(ms + EPS) * g_ref[...]).astype(o_ref.dtype)


def rmsnorm(x, g, out_dtype, tm=256):
    t, d = x.shape
    tm = min(tm, t)
    return pl.pallas_call(
        _rms_kernel, grid=(t // tm,),
        in_specs=[pl.BlockSpec((tm, d), lambda i: (i, 0)), pl.BlockSpec((1, d), lambda i: (0, 0))],
        out_specs=pl.BlockSpec((tm, d), lambda i: (i, 0)),
        out_shape=jax.ShapeDtypeStruct((t, d), out_dtype),
        compiler_params=_params(), name="rmsnorm")(x, g.reshape(1, d))


def _mm_kernel(x_ref, w_ref, o_ref):
    o_ref[...] = jnp.dot(x_ref[...], w_ref[...], preferred_element_type=F32).astype(o_ref.dtype)


def _mm_res_kernel(x_ref, w_ref, r_ref, o_ref):
    o_ref[...] = r_ref[...] + jnp.dot(x_ref[...], w_ref[...], preferred_element_type=F32)


def _ple_kernel(x_ref, wg_ref, p_ref, wp_ref, r_ref, o_ref):
    gate = jax.nn.sigmoid(jnp.dot(x_ref[...], wg_ref[...], preferred_element_type=F32))
    pp = jnp.dot(p_ref[...], wp_ref[...], preferred_element_type=F32)
    o_ref[...] = r_ref[...] + gate * pp


def matmul(x, w, out_dtype, tm=1024, tn=512):
    t, k = x.shape
    n = w.shape[1]
    tm, tn = min(tm, t), min(tn, n)
    return pl.pallas_call(
        _mm_kernel, grid=(t // tm, n // tn),
        in_specs=[pl.BlockSpec((tm, k), lambda i, j: (i, 0)), pl.BlockSpec((k, tn), lambda i, j: (0, j))],
        out_specs=pl.BlockSpec((tm, tn), lambda i, j: (i, j)),
        out_shape=jax.ShapeDtypeStruct((t, n), out_dtype),
        compiler_params=_params(2), name="matmul")(x, w)


def matmul_residual(x, w, res, tm=1024, tn=512):
    t, k = x.shape
    n = w.shape[1]
    tm, tn = min(tm, t), min(tn, n)
    return pl.pallas_call(
        _mm_res_kernel, grid=(t // tm, n // tn),
        in_specs=[pl.BlockSpec((tm, k), lambda i, j: (i, 0)), pl.BlockSpec((k, tn), lambda i, j: (0, j)),
                  pl.BlockSpec((tm, tn), lambda i, j: (i, j))],
        out_specs=pl.BlockSpec((tm, tn), lambda i, j: (i, j)),
        out_shape=jax.ShapeDtypeStruct((t, n), F32),
        compiler_params=_params(2), name="matmul_residual")(x, w, res)


def ple_update(xn, wg, p, wp, res, tm=1024, tn=512):
    t, k = xn.shape
    n = wg.shape[1]
    r = p.shape[1]
    tm, tn = min(tm, t), min(tn, n)
    return pl.pallas_call(
        _ple_kernel, grid=(t // tm, n // tn),
        in_specs=[pl.BlockSpec((tm, k), lambda i, j: (i, 0)), pl.BlockSpec((k, tn), lambda i, j: (0, j)),
                  pl.BlockSpec((tm, r), lambda i, j: (i, 0)), pl.BlockSpec((r, tn), lambda i, j: (0, j)),
                  pl.BlockSpec((tm, tn), lambda i, j: (i, j))],
        out_specs=pl.BlockSpec((tm, tn), lambda i, j: (i, j)),
        out_shape=jax.ShapeDtypeStruct((t, n), F32),
        compiler_params=_params(2), name="ple_update")(xn, wg, p, wp, res)


def _layer_norm(x, g, b):
    mu = jnp.mean(x, axis=-1, keepdims=True)
    xc = x - mu
    var = jnp.mean(xc * xc, axis=-1, keepdims=True)
    return xc * lax.rsqrt(var + EPS) * g + b


def _mixer_kernel(proj_ref, c3w_ref, lng_ref, lnb_ref, ws_ref, bexp_ref, cwb_ref, cb_ref, clg_ref, clb_ref,
                  y_ref, cx_ext, g_blk, conv_s):
    tm = y_ref.shape[0]

    @pl.when(pl.program_id(0) == 0)
    def _():
        cx_ext[0:SUBLANES, :] = jnp.zeros((SUBLANES, A_W), F32)
        g_blk[0, 0:CFM_HALO, :] = jnp.zeros((CFM_HALO, C_W), F32)

    a_c = proj_ref[:, OFF_AC:OFF_AC + A_W].astype(F32)
    a_x = proj_ref[:, OFF_AX:OFF_AX + A_W].astype(F32)
    cx_ext[SUBLANES:SUBLANES + tm, :] = a_c * a_x
    conv = c3w_ref[0:1, :] * cx_ext[SUBLANES - 2:SUBLANES - 2 + tm, :]
    conv += c3w_ref[1:2, :] * cx_ext[SUBLANES - 1:SUBLANES - 1 + tm, :]
    conv += c3w_ref[2:3, :] * cx_ext[SUBLANES:SUBLANES + tm, :]
    a_b = proj_ref[:, OFF_AB:OFF_AB + A_W].astype(F32)
    y_ref[:, 0:A_W] = (a_b * conv).astype(y_ref.dtype)
    cx_ext[0:SUBLANES, :] = cx_ext[tm:tm + SUBLANES, :]

    for r0 in range(0, tm, CHUNK):
        u = proj_ref[r0:r0 + CHUNK, OFF_BU:OFF_BU + B_W].astype(F32)
        v = proj_ref[r0:r0 + CHUNK, OFF_BV:OFF_BV + B_W].astype(F32)
        zu = jax.nn.gelu(u)
        zv = _layer_norm(jax.nn.gelu(v), lng_ref[...], lnb_ref[...]).astype(BF16)
        for h in range(N_B_HEADS):
            c0 = h * HEAD_DIM
            s = jnp.dot(ws_ref[h], zv[:, c0:c0 + HEAD_DIM], preferred_element_type=F32)
            s = s + bexp_ref[:, c0:c0 + HEAD_DIM]
            y_ref[r0:r0 + CHUNK, A_W + c0:A_W + c0 + HEAD_DIM] = (zu[:, c0:c0 + HEAD_DIM] * s).astype(y_ref.dtype)

    n_blk = tm // CFM_HALO
    for j in range(n_blk):
        r0 = j * CFM_HALO
        c_a = proj_ref[r0:r0 + CFM_HALO, OFF_CA:OFF_CA + C_W].astype(F32)
        c_g = proj_ref[r0:r0 + CFM_HALO, OFF_CG:OFF_CG + C_W].astype(F32)
        glu = c_a * jax.nn.sigmoid(c_g)
        g_blk[j, CFM_HALO:2 * CFM_HALO, :] = glu
        if j + 1 < n_blk:
            g_blk[j + 1, 0:CFM_HALO, :] = glu
    first_tap_row = CFM_HALO - (CFM_K - 1)

    def conv_rows(rb, carry):
        base = pl.multiple_of(rb * CFM_HALO, CFM_HALO)
        for c0 in range(0, C_W, HEAD_DIM):
            accs = [jnp.zeros((SUBLANES, HEAD_DIM), F32) for _ in range(CFM_HALO // SUBLANES)]
            for k in range(CFM_K):
                wv = cwb_ref[k * SUBLANES:(k + 1) * SUBLANES, c0:c0 + HEAD_DIM]
                for q in range(len(accs)):
                    r = q * SUBLANES + first_tap_row + k
                    accs[q] = accs[q] + wv * g_blk[rb, r:r + SUBLANES, c0:c0 + HEAD_DIM]
            for q in range(len(accs)):
                conv_s[pl.ds(base + q * SUBLANES, SUBLANES), c0:c0 + HEAD_DIM] = accs[q]
        blk = conv_s[pl.ds(base, CFM_HALO), :] + cb_ref[...]
        z = _layer_norm(blk, clg_ref[...], clb_ref[...])
        y_ref[pl.ds(base, CFM_HALO), A_W + B_W:MIX_W] = jax.nn.silu(z).astype(y_ref.dtype)
        return carry

    lax.fori_loop(0, n_blk, conv_rows, 0)
    g_blk[0, 0:CFM_HALO, :] = g_blk[n_blk - 1, CFM_HALO:2 * CFM_HALO, :]


def mixers(proj, c3w, lng, lnb, ws, bexp, cwb, cb, clg, clb, tm=256):
    t = proj.shape[0]
    tm = min(tm, t)
    full = lambda a: pl.BlockSpec(a.shape, lambda i: (0,) * a.ndim)
    args = (c3w, lng, lnb, ws, bexp, cwb, cb, clg, clb)
    return pl.pallas_call(
        _mixer_kernel, grid=(t // tm,),
        in_specs=[pl.BlockSpec((tm, IN_COLS), lambda i: (i, 0))] + [full(a) for a in args],
        out_specs=pl.BlockSpec((tm, MIX_W), lambda i: (i, 0)),
        out_shape=jax.ShapeDtypeStruct((t, MIX_W), BF16),
        scratch_shapes=[pltpu.VMEM((SUBLANES + tm, A_W), F32), pltpu.VMEM((tm // CFM_HALO, 2 * CFM_HALO, C_W), F32),
                        pltpu.VMEM((tm, C_W), F32)],
        compiler_params=_params(), name="mixers")(proj, *args)


def _router_kernel(h_ref, g_ref, w_ref, b_ref, o_ref):
    x = h_ref[...]
    ms = jnp.mean(x * x, axis=-1, keepdims=True)
    hn = (x * lax.rsqrt(ms + EPS) * g_ref[...]).astype(BF16)
    logits = jnp.dot(hn, w_ref[...], preferred_element_type=F32) + b_ref[...]
    lane = lax.broadcasted_iota(jnp.int32, logits.shape, 1)
    neg = jnp.float32(-jnp.inf)
    is_g = lane < N_GROUPS
    gl = jnp.where(is_g, logits, neg)
    gmax = jnp.max(gl, axis=-1, keepdims=True)
    gidx = jnp.min(jnp.where(gl == gmax, lane, ROUTER_LANES), axis=-1, keepdims=True)
    gsum = jnp.sum(jnp.where(is_g, jnp.exp(gl - gmax), 0.0), axis=-1, keepdims=True)
    g_p = 1.0 / gsum
    e_lane = lane - N_GROUPS
    in_grp = (e_lane >= 0) & (e_lane < N_EXPERTS) & (lax.shift_right_arithmetic(e_lane, 3) == gidx)
    el = jnp.where(in_grp, logits, neg)
    v1 = jnp.max(el, axis=-1, keepdims=True)
    i1 = jnp.min(jnp.where(el == v1, lane, ROUTER_LANES), axis=-1, keepdims=True)
    el2 = jnp.where(lane == i1, neg, el)
    v2 = jnp.max(el2, axis=-1, keepdims=True)
    i2 = jnp.min(jnp.where(el2 == v2, lane, ROUTER_LANES), axis=-1, keepdims=True)
    t2 = jnp.exp(v2 - v1)
    w1 = g_p / (1.0 + t2)
    w2 = w1 * t2
    out = jnp.where(lane == 0, w1, 0.0)
    out = jnp.where(lane == 1, w2, out)
    out = jnp.where(lane == 2, (i1 - N_GROUPS).astype(F32), out)
    out = jnp.where(lane == 3, (i2 - N_GROUPS).astype(F32), out)
    o_ref[...] = out


def router(h, g, w, b, tm=256):
    t, d = h.shape
    tm = min(tm, t)
    return pl.pallas_call(
        _router_kernel, grid=(t // tm,),
        in_specs=[pl.BlockSpec((tm, d), lambda i: (i, 0)), pl.BlockSpec((1, d), lambda i: (0, 0)),
                  pl.BlockSpec((d, ROUTER_LANES), lambda i: (0, 0)), pl.BlockSpec((1, ROUTER_LANES), lambda i: (0, 0))],
        out_specs=pl.BlockSpec((tm, ROUTER_LANES), lambda i: (i, 0)),
        out_shape=jax.ShapeDtypeStruct((t, ROUTER_LANES), F32),
        compiler_params=_params(), name="router")(h, g.reshape(1, d), w, b)


def _row_copy(src_hbm, dst_vmem, src_row, dst_row, sem):
    return pltpu.make_async_copy(src_hbm.at[pl.ds(src_row, 1), :], dst_vmem.at[pl.ds(dst_row, 1), :], sem)


def _moe_kernel(te_ref, nt_ref, rt_ref, h_hbm, g_ref, gate_ref, w1_ref, w3_ref, w2_ref, o_ref, xbuf, sem):
    tm = xbuf.shape[0]
    i = pl.program_id(0)

    @pl.when(i < nt_ref[0])
    def _():
        base = i * tm

        def issue(r, c):
            _row_copy(h_hbm, xbuf, rt_ref[base + r], r, sem).start()
            return c

        lax.fori_loop(0, tm, issue, 0)

        def drain(r, c):
            _row_copy(h_hbm, xbuf, 0, r, sem).wait()
            return c

        lax.fori_loop(0, tm, drain, 0)
        x = xbuf[...]
        ms = jnp.mean(x * x, axis=-1, keepdims=True)
        hn = (x * lax.rsqrt(ms + EPS) * g_ref[...]).astype(BF16)
        h1 = jnp.dot(hn, w1_ref[0], preferred_element_type=F32)
        h3 = jnp.dot(hn, w3_ref[0], preferred_element_type=F32)
        hid = (jax.nn.silu(h1) * h3 * gate_ref[...]).astype(BF16)
        o_ref[...] = jnp.dot(hid, w2_ref[0], preferred_element_type=F32)

    @pl.when(i >= nt_ref[0])
    def _():
        o_ref[...] = jnp.zeros(o_ref.shape, o_ref.dtype)


def moe_experts(h, g, tile_expert, n_tiles, row_token, row_gate, w1, w3, w2, tm):
    t, d = h.shape
    nt = tile_expert.shape[0]
    ff = w1.shape[2]
    last = lambda i, te, n, rt: jnp.minimum(i, n[0] - 1)
    grid_spec = pltpu.PrefetchScalarGridSpec(
        num_scalar_prefetch=3, grid=(nt,),
        in_specs=[pl.BlockSpec(memory_space=pl.ANY),
                  pl.BlockSpec((1, d), lambda i, te, n, rt: (0, 0)),
                  pl.BlockSpec((tm, 1), lambda i, te, n, rt: (last(i, te, n, rt), 0)),
                  pl.BlockSpec((1, d, ff), lambda i, te, n, rt: (te[i], 0, 0)),
                  pl.BlockSpec((1, d, ff), lambda i, te, n, rt: (te[i], 0, 0)),
                  pl.BlockSpec((1, ff, d), lambda i, te, n, rt: (te[i], 0, 0))],
        out_specs=pl.BlockSpec((tm, d), lambda i, te, n, rt: (i, 0)),
        scratch_shapes=[pltpu.VMEM((tm, d), F32), pltpu.SemaphoreType.DMA(())])
    return pl.pallas_call(
        _moe_kernel, grid_spec=grid_spec,
        out_shape=jax.ShapeDtypeStruct((nt * tm, d), F32),
        compiler_params=_params(), name="moe_experts")(
            tile_expert, n_tiles, row_token, h, g.reshape(1, d), row_gate, w1, w3, w2)


def _combine_kernel(pos_ref, h_ref, ys_hbm, o_ref, buf, sem):
    tm = h_ref.shape[0]
    base = pl.program_id(0) * tm

    def issue(r, c):
        for k in range(2):
            _row_copy(ys_hbm, buf.at[k], pos_ref[2 * (base + r) + k], r, sem).start()
        return c

    lax.fori_loop(0, tm, issue, 0)

    def drain(r, c):
        for k in range(2):
            _row_copy(ys_hbm, buf.at[k], 0, r, sem).wait()
        return c

    lax.fori_loop(0, tm, drain, 0)
    o_ref[...] = h_ref[...] + buf[0] + buf[1]


def moe_combine(h, ys, pos, tm=128):
    t, d = h.shape
    tm = min(tm, t)
    grid_spec = pltpu.PrefetchScalarGridSpec(
        num_scalar_prefetch=1, grid=(t // tm,),
        in_specs=[pl.BlockSpec((tm, d), lambda i, p: (i, 0)), pl.BlockSpec(memory_space=pl.ANY)],
        out_specs=pl.BlockSpec((tm, d), lambda i, p: (i, 0)),
        scratch_shapes=[pltpu.VMEM((2, tm, d), F32), pltpu.SemaphoreType.DMA(())])
    return pl.pallas_call(
        _combine_kernel, grid_spec=grid_spec,
        out_shape=jax.ShapeDtypeStruct((t, d), F32),
        compiler_params=_params(), name="moe_combine")(pos, h, ys)


def _dispatch_tables(eid, wts, tm):
    t = eid.shape[0]
    n_pairs = 2 * t
    nt = n_pairs // tm + N_EXPERTS
    e_flat = eid.reshape(-1)
    order = jnp.argsort(e_flat, stable=True).astype(jnp.int32)
    sorted_e = e_flat[order]
    counts = jnp.sum(jax.nn.one_hot(e_flat, N_EXPERTS, dtype=jnp.int32), axis=0)
    padded = ((counts + tm - 1) // tm) * tm
    pad_end = jnp.cumsum(padded)
    pad_off = pad_end - padded
    off = jnp.cumsum(counts) - counts
    dest = pad_off[sorted_e] + (jnp.arange(n_pairs, dtype=jnp.int32) - off[sorted_e])
    row_token = jnp.zeros((nt * tm,), jnp.int32).at[dest].set(order // 2)
    row_gate = jnp.zeros((nt * tm,), F32).at[dest].set(wts.reshape(-1)[order])
    pos = jnp.zeros((n_pairs,), jnp.int32).at[order].set(dest)
    n_tiles = (pad_end[-1] // tm).astype(jnp.int32)
    tile_start = jnp.arange(nt, dtype=jnp.int32) * tm
    tile_expert = jnp.searchsorted(pad_end, tile_start, side="right").astype(jnp.int32)
    last_e = jnp.searchsorted(pad_end, (n_tiles - 1) * tm, side="right").astype(jnp.int32)
    tile_expert = jnp.where(tile_start < pad_end[-1], tile_expert, last_e)
    return tile_expert, n_tiles.reshape(1), row_token, row_gate.reshape(-1, 1), pos


def kernel(x, p, norm_mix, w_in, conv3_w, sgu_ln_g, sgu_ln_b, sgu_w, sgu_b, cfm_conv_w, cfm_conv_b, cfm_ln_g,
           cfm_ln_b, w_out, norm_moe, router_group_w, router_group_b, router_expert_w, router_expert_b,
           exp_w1, exp_w3, exp_w2, norm_ple, ple_gate_w, ple_proj_w, final_norm):
    bsz, seq, d = x.shape
    depth = w_in.shape[0]
    t = bsz * seq
    assert bsz == 1, "conv history is carried across row tiles of a single sequence"
    moe_tm = min(256, t)
    h = x.reshape(t, d)
    tril = jnp.tril(jnp.ones((CHUNK, CHUNK), dtype=bool))
    pad_lanes = ROUTER_LANES - N_GROUPS - N_EXPERTS
    for i in range(depth):
        hn = rmsnorm(h, norm_mix[i], BF16)
        proj = matmul(hn, w_in[i].astype(BF16), BF16)
        ws = jnp.where(tril[None], sgu_w[i], 0.0).astype(BF16)
        bexp = jnp.repeat(jnp.transpose(sgu_b[i]), HEAD_DIM, axis=1)
        cwb = jnp.repeat(cfm_conv_w[i], SUBLANES, axis=0)
        y = mixers(proj, conv3_w[i], sgu_ln_g[i].reshape(1, -1), sgu_ln_b[i].reshape(1, -1), ws, bexp, cwb,
                   cfm_conv_b[i].reshape(1, -1), cfm_ln_g[i].reshape(1, -1), cfm_ln_b[i].reshape(1, -1))
        h = matmul_residual(y, w_out[i].astype(BF16), h)
        wr = jnp.concatenate([router_group_w[i], router_expert_w[i], jnp.zeros((d, pad_lanes), F32)], axis=1)
        br = jnp.concatenate([router_group_b[i], router_expert_b[i], jnp.zeros((pad_lanes,), F32)])
        route = router(h, norm_moe[i], wr.astype(BF16), br.reshape(1, -1))
        wts = route[:, 0:2]
        eid = route[:, 2:4].astype(jnp.int32)
        tile_expert, n_tiles, row_token, row_gate, pos = _dispatch_tables(eid, wts, moe_tm)
        ys = moe_experts(h, norm_moe[i], tile_expert, n_tiles, row_token, row_gate,
                         exp_w1[i].astype(BF16), exp_w3[i].astype(BF16), exp_w2[i].astype(BF16), moe_tm)
        h = moe_combine(h, ys, pos)
        hn = rmsnorm(h, norm_ple[i], BF16)
        h = ple_update(hn, ple_gate_w[i].astype(BF16), p[i].reshape(t, -1).astype(BF16),
                       ple_proj_w[i].astype(BF16), h)
    return rmsnorm(h, final_norm, F32).reshape(bsz, seq, d)
```

```python
import functools

import jax
import jax.numpy as jnp
from jax import lax
from jax.experimental import pallas as pl
from jax.experimental.pallas import tpu as pltpu

F32 = jnp.float32
BF16 = jnp.bfloat16
U32 = jnp.uint32
EPS = 1e-6

LANES = 128
SUBLANES = 8
HEAD_DIM = 128
CHUNK = 128
N_B_HEADS = 8
A_W = 12 * HEAD_DIM
B_W = N_B_HEADS * HEAD_DIM
C_W = 12 * HEAD_DIM
MIX_W = A_W + B_W + C_W
OFF_AB, OFF_AC, OFF_AX = 0, A_W, 2 * A_W
OFF_BU, OFF_BV = 3 * A_W, 3 * A_W + B_W
OFF_CA, OFF_CG = 3 * A_W + 2 * B_W, 3 * A_W + 2 * B_W + C_W
IN_COLS = 3 * A_W + 2 * B_W + 2 * C_W
SHORT_K = 3
CFM_K = 31
CFM_HALO = 32
CONV_STRIDE = 4
CONV_ROWS = SUBLANES * CONV_STRIDE
N_GROUPS = 4
EXPERTS_PER_GROUP = 8
N_EXPERTS = N_GROUPS * EXPERTS_PER_GROUP
ROUTER_LANES = LANES
VMEM_LIMIT = 56 * 1024 * 1024


def _params(n_axes=1):
    return pltpu.CompilerParams(dimension_semantics=("arbitrary",) * n_axes,
                                vmem_limit_bytes=VMEM_LIMIT)


def _slab_dims(d):
    words = d // 2
    rows = words // LANES
    return words, rows, rows + 4


def _pack_rows(x):
    half = x.shape[1] // 2
    hi = lax.bitcast_convert_type(x[:, :half].astype(BF16).astype(F32), U32)
    lo = lax.bitcast_convert_type(x[:, half:].astype(BF16).astype(F32), U32)
    return hi | (lo >> 16)


def _unpack_rows(w):
    hi = lax.bitcast_convert_type(w & jnp.uint32(0xFFFF0000), F32)
    lo = lax.bitcast_convert_type(w << 16, F32)
    return hi, lo


def _store_slabs(ref, w, m, rows, pitch):
    for s in range(rows):
        ref[pl.ds(s, m, stride=pitch), :] = w[:, s * LANES:(s + 1) * LANES]
    for s in range(rows, pitch):
        ref[pl.ds(s, m, stride=pitch), :] = jnp.zeros((m, LANES), w.dtype)


def _load_slabs(ref, lead, row0, m, rows, pitch):
    return jnp.concatenate([ref[lead, pl.ds(row0 + s, m, stride=pitch), :] for s in range(rows)], axis=1)


def _rms_kernel(x_ref, g_ref, o_ref):
    x = x_ref[...]
    ms = jnp.mean(x * x, axis=-1, keepdims=True)
    o_ref[...] = (x * lax.rsqrt(ms + EPS) * g_ref[...]).astype(o_ref.dtype)


def rmsnorm(x, g, out_dtype, tm=256):
    t, d = x.shape
    tm = min(tm, t)
    return pl.pallas_call(
        _rms_kernel, grid=(t // tm,),
        in_specs=[pl.BlockSpec((tm, d), lambda i: (i, 0)), pl.BlockSpec((1, d), lambda i: (0, 0))],
        out_specs=pl.BlockSpec((tm, d), lambda i: (i, 0)),
        out_shape=jax.ShapeDtypeStruct((t, d), out_dtype),
        compiler_params=_params(), name="rmsnorm")(x, g.reshape(1, d))


def _in_proj_kernel(x_ref, w_ref, g_ref, o_ref, wb, rs):
    j, i = pl.program_id(0), pl.program_id(1)

    @pl.when(i == 0)
    def _():
        wb[...] = (w_ref[...] * g_ref[...]).astype(BF16)

    @pl.when(j == 0)
    def _():
        xf = x_ref[...].astype(F32)
        rs[i] = lax.rsqrt(jnp.mean(xf * xf, axis=-1, keepdims=True) + EPS)

    acc = jnp.dot(x_ref[...], wb[...], preferred_element_type=F32)
    o_ref[...] = (acc * rs[i]).astype(o_ref.dtype)


def in_proj(hb, w, g, tm=1024, tn=512):
    t, k = hb.shape
    n = w.shape[1]
    tm, tn = min(tm, t), min(tn, n)
    return pl.pallas_call(
        _in_proj_kernel, grid=(n // tn, t // tm),
        in_specs=[pl.BlockSpec((tm, k), lambda j, i: (i, 0)), pl.BlockSpec((k, tn), lambda j, i: (0, j)),
                  pl.BlockSpec((k, 1), lambda j, i: (0, 0))],
        out_specs=pl.BlockSpec((tm, tn), lambda j, i: (i, j)),
        out_shape=jax.ShapeDtypeStruct((t, n), BF16),
        scratch_shapes=[pltpu.VMEM((k, tn), BF16), pltpu.VMEM((t // tm, tm, 1), F32)],
        compiler_params=_params(2), name="in_proj")(hb, w, g.reshape(k, 1))


def _out_proj_kernel(y_ref, w_ref, r_ref, o_ref, wb):
    @pl.when(pl.program_id(1) == 0)
    def _():
        wb[...] = w_ref[...].astype(BF16)

    o_ref[...] = r_ref[...] + jnp.dot(y_ref[...], wb[...], preferred_element_type=F32)


def out_proj(y, w, res, tm=1024, tn=512):
    t, k = y.shape
    n = w.shape[1]
    tm, tn = min(tm, t), min(tn, n)
    return pl.pallas_call(
        _out_proj_kernel, grid=(n // tn, t // tm),
        in_specs=[pl.BlockSpec((tm, k), lambda j, i: (i, 0)), pl.BlockSpec((k, tn), lambda j, i: (0, j)),
                  pl.BlockSpec((tm, tn), lambda j, i: (i, j))],
        out_specs=pl.BlockSpec((tm, tn), lambda j, i: (i, j)),
        out_shape=jax.ShapeDtypeStruct((t, n), F32),
        scratch_shapes=[pltpu.VMEM((k, tn), BF16)],
        compiler_params=_params(2), name="out_proj")(y, w, res)


def _ple_kernel(x_ref, wg_ref, p_ref, wp_ref, r_ref, o_ref, ob_ref, wgb, wpb):
    @pl.when(pl.program_id(1) == 0)
    def _():
        wgb[...] = wg_ref[...].astype(BF16)
        wpb[...] = wp_ref[...].astype(BF16)

    gate = jax.nn.sigmoid(jnp.dot(x_ref[...], wgb[...], preferred_element_type=F32))
    pp = jnp.dot(p_ref[...].astype(BF16), wpb[...], preferred_element_type=F32)
    h = r_ref[...] + gate * pp
    o_ref[...] = h
    ob_ref[...] = h.astype(BF16)


def ple_update(xn, wg, p, wp, res, tm=1024, tn=512):
    t, k = xn.shape
    n = wg.shape[1]
    r = p.shape[1]
    tm, tn = min(tm, t), min(tn, n)
    return pl.pallas_call(
        _ple_kernel, grid=(n // tn, t // tm),
        in_specs=[pl.BlockSpec((tm, k), lambda j, i: (i, 0)), pl.BlockSpec((k, tn), lambda j, i: (0, j)),
                  pl.BlockSpec((tm, r), lambda j, i: (i, 0)), pl.BlockSpec((r, tn), lambda j, i: (0, j)),
                  pl.BlockSpec((tm, tn), lambda j, i: (i, j))],
        out_specs=[pl.BlockSpec((tm, tn), lambda j, i: (i, j)), pl.BlockSpec((tm, tn), lambda j, i: (i, j))],
        out_shape=[jax.ShapeDtypeStruct((t, n), F32), jax.ShapeDtypeStruct((t, n), BF16)],
        scratch_shapes=[pltpu.VMEM((k, tn), BF16), pltpu.VMEM((r, tn), BF16)],
        compiler_params=_params(2), name="ple_update")(xn, wg, p, wp, res)


def _layer_norm(x, g, b):
    mu = jnp.mean(x, axis=-1, keepdims=True)
    xc = x - mu
    var = jnp.mean(xc * xc, axis=-1, keepdims=True)
    return xc * lax.rsqrt(var + EPS) * g + b


def _mixer_kernel(proj_ref, c3w_ref, lng_ref, lnb_ref, ws_ref, bexp_ref, cwb_ref, cb_ref, clg_ref, clb_ref,
                  y_ref, cx_ext, g_s, conv_s):
    tm = y_ref.shape[0]
    n_slab = C_W // LANES

    @pl.when(pl.program_id(0) == 0)
    def _():
        cx_ext[0:SUBLANES, :] = jnp.zeros((SUBLANES, A_W), F32)
        g_s[:, 0:CFM_HALO, :] = jnp.zeros((n_slab, CFM_HALO, LANES), F32)

    a_c = proj_ref[:, OFF_AC:OFF_AC + A_W].astype(F32)
    a_x = proj_ref[:, OFF_AX:OFF_AX + A_W].astype(F32)
    cx_ext[SUBLANES:SUBLANES + tm, :] = a_c * a_x
    conv = c3w_ref[0:1, :] * cx_ext[SUBLANES - 2:SUBLANES - 2 + tm, :]
    conv += c3w_ref[1:2, :] * cx_ext[SUBLANES - 1:SUBLANES - 1 + tm, :]
    conv += c3w_ref[2:3, :] * cx_ext[SUBLANES:SUBLANES + tm, :]
    a_b = proj_ref[:, OFF_AB:OFF_AB + A_W].astype(F32)
    y_ref[:, 0:A_W] = (a_b * conv).astype(y_ref.dtype)
    cx_ext[0:SUBLANES, :] = cx_ext[tm:tm + SUBLANES, :]

    for r0 in range(0, tm, CHUNK):
        u = proj_ref[r0:r0 + CHUNK, OFF_BU:OFF_BU + B_W].astype(F32)
        v = proj_ref[r0:r0 + CHUNK, OFF_BV:OFF_BV + B_W].astype(F32)
        zu = jax.nn.gelu(u)
        zv = _layer_norm(jax.nn.gelu(v), lng_ref[...], lnb_ref[...]).astype(BF16)
        for h in range(N_B_HEADS):
            c0 = h * HEAD_DIM
            s = jnp.dot(ws_ref[h], zv[:, c0:c0 + HEAD_DIM], preferred_element_type=F32)
            s = s + bexp_ref[:, c0:c0 + HEAD_DIM]
            y_ref[r0:r0 + CHUNK, A_W + c0:A_W + c0 + HEAD_DIM] = (zu[:, c0:c0 + HEAD_DIM] * s).astype(y_ref.dtype)

    c_a = proj_ref[:, OFF_CA:OFF_CA + C_W].astype(F32)
    c_g = proj_ref[:, OFF_CG:OFF_CG + C_W].astype(F32)
    glu = c_a * jax.nn.sigmoid(c_g)
    for c in range(n_slab):
        g_s[c, CFM_HALO:CFM_HALO + tm, :] = glu[:, c * LANES:(c + 1) * LANES]
    first_tap_row = CFM_HALO - (CFM_K - 1)

    def conv_rows(rb, carry):
        base = pl.multiple_of(rb * CONV_ROWS, CONV_ROWS)
        for c in range(n_slab):
            accs = [jnp.zeros((SUBLANES, LANES), F32) for _ in range(CONV_STRIDE)]
            for k in range(CFM_K):
                wv = cwb_ref[k * SUBLANES:(k + 1) * SUBLANES, c * LANES:(c + 1) * LANES]
                for j in range(CONV_STRIDE):
                    rows = pl.ds(base + (first_tap_row + k + j), SUBLANES, stride=CONV_STRIDE)
                    accs[j] = accs[j] + wv * g_s[c, rows, :]
            for j in range(CONV_STRIDE):
                conv_s[c, pl.ds(base + j, SUBLANES, stride=CONV_STRIDE), :] = accs[j]
        blk = jnp.concatenate([conv_s[c, pl.ds(base, CONV_ROWS), :] for c in range(n_slab)], axis=1) + cb_ref[...]
        z = _layer_norm(blk, clg_ref[...], clb_ref[...])
        y_ref[pl.ds(base, CONV_ROWS), A_W + B_W:MIX_W] = jax.nn.silu(z).astype(y_ref.dtype)
        return carry

    lax.fori_loop(0, tm // CONV_ROWS, conv_rows, 0)
    g_s[:, 0:CFM_HALO, :] = g_s[:, tm:tm + CFM_HALO, :]


def mixers(proj, c3w, lng, lnb, ws, bexp, cwb, cb, clg, clb, tm=256):
    t = proj.shape[0]
    tm = min(tm, t)
    full = lambda a: pl.BlockSpec(a.shape, lambda i: (0,) * a.ndim)
    args = (c3w, lng, lnb, ws, bexp, cwb, cb, clg, clb)
    n_slab = C_W // LANES
    return pl.pallas_call(
        _mixer_kernel, grid=(t // tm,),
        in_specs=[pl.BlockSpec((tm, IN_COLS), lambda i: (i, 0))] + [full(a) for a in args],
        out_specs=pl.BlockSpec((tm, MIX_W), lambda i: (i, 0)),
        out_shape=jax.ShapeDtypeStruct((t, MIX_W), BF16),
        scratch_shapes=[pltpu.VMEM((SUBLANES + tm, A_W), F32), pltpu.VMEM((n_slab, CFM_HALO + tm, LANES), F32),
                        pltpu.VMEM((n_slab, tm, LANES), F32)],
        compiler_params=_params(), name="mixers")(proj, *args)


def _router_kernel(h_ref, g_ref, w_ref, b_ref, o_ref, xs_ref, cnt_ref):
    tm, d = h_ref.shape
    x = h_ref[...]
    ms = jnp.mean(x * x, axis=-1, keepdims=True)
    hn = x * lax.rsqrt(ms + EPS) * g_ref[...]
    logits = jnp.dot(hn.astype(BF16), w_ref[...].astype(BF16), preferred_element_type=F32) + b_ref[...]
    lane = lax.broadcasted_iota(jnp.int32, logits.shape, 1)
    neg = jnp.float32(-jnp.inf)
    is_g = lane < N_GROUPS
    gl = jnp.where(is_g, logits, neg)
    gmax = jnp.max(gl, axis=-1, keepdims=True)
    gidx = jnp.min(jnp.where(gl == gmax, lane, ROUTER_LANES), axis=-1, keepdims=True)
    gsum = jnp.sum(jnp.where(is_g, jnp.exp(gl - gmax), 0.0), axis=-1, keepdims=True)
    g_p = 1.0 / gsum
    e_lane = lane - N_GROUPS
    in_grp = (e_lane >= 0) & (e_lane < N_EXPERTS) & (lax.shift_right_arithmetic(e_lane, 3) == gidx)
    el = jnp.where(in_grp, logits, neg)
    v1 = jnp.max(el, axis=-1, keepdims=True)
    i1 = jnp.min(jnp.where(el == v1, lane, ROUTER_LANES), axis=-1, keepdims=True)
    el2 = jnp.where(lane == i1, neg, el)
    v2 = jnp.max(el2, axis=-1, keepdims=True)
    i2 = jnp.min(jnp.where(el2 == v2, lane, ROUTER_LANES), axis=-1, keepdims=True)
    t2 = jnp.exp(v2 - v1)
    w1 = g_p / (1.0 + t2)
    w2 = w1 * t2
    out = jnp.where(lane == 0, w1, 0.0)
    out = jnp.where(lane == 1, w2, out)
    out = jnp.where(lane == 2, (i1 - N_GROUPS).astype(F32), out)
    out = jnp.where(lane == 3, (i2 - N_GROUPS).astype(F32), out)
    o_ref[...] = out
    picks = jnp.where(lane == i1, 1.0, 0.0) + jnp.where(lane == i2, 1.0, 0.0)
    cnt_ref[...] = jnp.broadcast_to(jnp.sum(picks, axis=0, keepdims=True), cnt_ref.shape)
    _, rows, pitch = _slab_dims(d)
    _store_slabs(xs_ref, _pack_rows(hn), tm, rows, pitch)


def router(h, g, w, b, tm=256):
    t, d = h.shape
    tm = min(tm, t)
    _, _, pitch = _slab_dims(d)
    return pl.pallas_call(
        _router_kernel, grid=(t // tm,),
        in_specs=[pl.BlockSpec((tm, d), lambda i: (i, 0)), pl.BlockSpec((1, d), lambda i: (0, 0)),
                  pl.BlockSpec((d, ROUTER_LANES), lambda i: (0, 0)), pl.BlockSpec((1, ROUTER_LANES), lambda i: (0, 0))],
        out_specs=[pl.BlockSpec((tm, ROUTER_LANES), lambda i: (i, 0)),
                   pl.BlockSpec((tm * pitch, LANES), lambda i: (i, 0)),
                   pl.BlockSpec((SUBLANES, ROUTER_LANES), lambda i: (i, 0))],
        out_shape=[jax.ShapeDtypeStruct((t, ROUTER_LANES), F32),
                   jax.ShapeDtypeStruct((t * pitch, LANES), U32),
                   jax.ShapeDtypeStruct((t // tm * SUBLANES, ROUTER_LANES), F32)],
        compiler_params=_params(), name="router")(h, g.reshape(1, d), w, b)


def _slab_copy(src_hbm, dst_vmem, src_slab, dst_slab, rows, pitch, sem):
    return pltpu.make_async_copy(src_hbm.at[pl.ds(src_slab * pitch, rows), :],
                                 dst_vmem.at[pl.ds(dst_slab * pitch, rows), :], sem)


def _moe_kernel(te_ref, tf_ref, nt_ref, rt_ref, xs_hbm, gate_ref, w1_ref, w3_ref, w2_ref, o_ref,
                xbuf, w1b, w3b, w2b, sems):
    d = w1_ref.shape[1]
    _, rows, pitch = _slab_dims(d)
    tm = xbuf.shape[1] // pitch
    i = pl.program_id(0)
    n_used = nt_ref[0]
    slot = lax.rem(i, 2)

    def gather(tile, dst_slot, start):
        base = tile * tm

        def body(r, c):
            src = rt_ref[base + r] if start else 0
            cp = _slab_copy(xs_hbm, xbuf.at[dst_slot], src, r, rows, pitch, sems.at[dst_slot])
            cp.start() if start else cp.wait()
            return c

        lax.fori_loop(0, tm, body, 0, unroll=8)

    @pl.when(i == 0)
    def _():
        gather(0, 0, True)

    @pl.when(i < n_used)
    def _():
        gather(i, slot, False)

        @pl.when(i + 1 < n_used)
        def _():
            gather(i + 1, 1 - slot, True)

        @pl.when(tf_ref[i] == 1)
        def _():
            w1b[...] = w1_ref[0].astype(BF16)
            w3b[...] = w3_ref[0].astype(BF16)
            w2b[...] = w2_ref[0].astype(BF16)

        hi, lo = _unpack_rows(_load_slabs(xbuf, slot, 0, tm, rows, pitch))
        hn = jnp.concatenate([hi.astype(BF16), lo.astype(BF16)], axis=1)
        h1 = jnp.dot(hn, w1b[...], preferred_element_type=F32)
        h3 = jnp.dot(hn, w3b[...], preferred_element_type=F32)
        hid = (jax.nn.silu(h1) * h3 * gate_ref[...]).astype(BF16)
        out = jnp.dot(hid, w2b[...], preferred_element_type=F32)
        _store_slabs(o_ref, _pack_rows(out), tm, rows, pitch)

    @pl.when(i >= n_used)
    def _():
        o_ref[...] = jnp.zeros(o_ref.shape, o_ref.dtype)


def moe_experts(xs, d, tile_expert, tile_first, n_tiles, row_slab, row_gate, w1, w3, w2, tm):
    nt = tile_expert.shape[0]
    ff = w1.shape[2]
    _, _, pitch = _slab_dims(d)
    last = lambda i, te, tf, n, rt: jnp.minimum(i, n[0] - 1)
    grid_spec = pltpu.PrefetchScalarGridSpec(
        num_scalar_prefetch=4, grid=(nt,),
        in_specs=[pl.BlockSpec(memory_space=pl.ANY),
                  pl.BlockSpec((tm, 1), lambda i, te, tf, n, rt: (last(i, te, tf, n, rt), 0)),
                  pl.BlockSpec((1, d, ff), lambda i, te, tf, n, rt: (te[i], 0, 0)),
                  pl.BlockSpec((1, d, ff), lambda i, te, tf, n, rt: (te[i], 0, 0)),
                  pl.BlockSpec((1, ff, d), lambda i, te, tf, n, rt: (te[i], 0, 0))],
        out_specs=pl.BlockSpec((tm * pitch, LANES), lambda i, te, tf, n, rt: (i, 0)),
        scratch_shapes=[pltpu.VMEM((2, tm * pitch, LANES), U32), pltpu.VMEM((d, ff), BF16),
                        pltpu.VMEM((d, ff), BF16), pltpu.VMEM((ff, d), BF16), pltpu.SemaphoreType.DMA((2,))])
    return pl.pallas_call(
        _moe_kernel, grid_spec=grid_spec,
        out_shape=jax.ShapeDtypeStruct((nt * tm * pitch, LANES), U32),
        compiler_params=_params(), name="moe_experts")(
            tile_expert, tile_first, n_tiles, row_slab, xs, row_gate, w1, w3, w2)


def _combine_kernel(pos_ref, h_ref, g_ref, ys_hbm, o_ref, on_ref, buf, sems):
    tm, d = h_ref.shape
    _, rows, pitch = _slab_dims(d)
    i = pl.program_id(0)
    slot = lax.rem(i, 2)

    def gather(tile, dst_slot, start):
        base = tile * tm

        def body(r, c):
            for k in range(2):
                src = pos_ref[2 * (base + r) + k] if start else 0
                cp = _slab_copy(ys_hbm, buf.at[dst_slot], src, k * tm + r, rows, pitch, sems.at[dst_slot])
                cp.start() if start else cp.wait()
            return c

        lax.fori_loop(0, tm, body, 0, unroll=4)

    @pl.when(i == 0)
    def _():
        gather(0, 0, True)

    gather(i, slot, False)

    @pl.when(i + 1 < pl.num_programs(0))
    def _():
        gather(i + 1, 1 - slot, True)

    h = h_ref[...]
    for k in range(2):
        hi, lo = _unpack_rows(_load_slabs(buf, slot, k * tm * pitch, tm, rows, pitch))
        h = h + jnp.concatenate([hi, lo], axis=1)
    o_ref[...] = h
    ms = jnp.mean(h * h, axis=-1, keepdims=True)
    on_ref[...] = (h * lax.rsqrt(ms + EPS) * g_ref[...]).astype(on_ref.dtype)


def moe_combine(h, g, ys, pos, tm=256):
    t, d = h.shape
    tm = min(tm, t)
    _, _, pitch = _slab_dims(d)
    grid_spec = pltpu.PrefetchScalarGridSpec(
        num_scalar_prefetch=1, grid=(t // tm,),
        in_specs=[pl.BlockSpec((tm, d), lambda i, p: (i, 0)), pl.BlockSpec((1, d), lambda i, p: (0, 0)),
                  pl.BlockSpec(memory_space=pl.ANY)],
        out_specs=[pl.BlockSpec((tm, d), lambda i, p: (i, 0)), pl.BlockSpec((tm, d), lambda i, p: (i, 0))],
        scratch_shapes=[pltpu.VMEM((2, 2 * tm * pitch, LANES), U32), pltpu.SemaphoreType.DMA((2,))])
    return pl.pallas_call(
        _combine_kernel, grid_spec=grid_spec,
        out_shape=[jax.ShapeDtypeStruct((t, d), F32), jax.ShapeDtypeStruct((t, d), BF16)],
        compiler_params=_params(), name="moe_combine")(pos, h, g.reshape(1, d), ys)


def _dispatch_tables(eid, wts, counts, tm):
    t = eid.shape[0]
    n_pairs = 2 * t
    nt = n_pairs // tm + N_EXPERTS
    e_flat = eid.reshape(-1)
    iota = jnp.arange(n_pairs, dtype=jnp.int32)
    _, order = lax.sort((e_flat, iota), num_keys=1, is_stable=True)
    _, inv = lax.sort((order, iota), num_keys=1)
    padded = ((counts + tm - 1) // tm) * tm
    pad_end = jnp.cumsum(padded)
    pad_off = pad_end - padded
    off = jnp.cumsum(counts) - counts
    n_tiles = pad_end[-1] // tm
    tile_idx = jnp.arange(nt, dtype=jnp.int32)
    tile_expert = jnp.sum((pad_end[None, :] <= (tile_idx * tm)[:, None]).astype(jnp.int32), axis=1)
    last_e = jnp.sum((pad_end <= (n_tiles - 1) * tm).astype(jnp.int32))
    tile_expert = jnp.where(tile_idx < n_tiles, tile_expert, last_e)
    prev = jnp.concatenate([jnp.full((1,), -1, jnp.int32), tile_expert[:-1]])
    tile_first = (tile_expert != prev).astype(jnp.int32)
    row = jnp.arange(nt * tm, dtype=jnp.int32)
    row_e = jnp.repeat(tile_expert, tm)
    local = row - pad_off[row_e]
    valid = (local < counts[row_e]) & (row < n_tiles * tm)
    src = order[jnp.clip(off[row_e] + local, 0, n_pairs - 1)]
    row_slab = jnp.where(valid, src // 2, 0)
    row_gate = jnp.where(valid, wts.reshape(-1)[src], 0.0)
    pos = pad_off[e_flat] + (inv - off[e_flat])
    return tile_expert, tile_first, n_tiles.reshape(1), row_slab, row_gate.reshape(-1, 1), pos


def kernel(x, p, norm_mix, w_in, conv3_w, sgu_ln_g, sgu_ln_b, sgu_w, sgu_b, cfm_conv_w, cfm_conv_b, cfm_ln_g,
           cfm_ln_b, w_out, norm_moe, router_group_w, router_group_b, router_expert_w, router_expert_b,
           exp_w1, exp_w3, exp_w2, norm_ple, ple_gate_w, ple_proj_w, final_norm):
    bsz, seq, d = x.shape
    depth = w_in.shape[0]
    t = bsz * seq
    assert bsz == 1, "conv history is carried across row tiles of a single sequence"
    moe_tm = min(256, t)
    h = x.reshape(t, d)
    hb = h.astype(BF16)
    tril = jnp.tril(jnp.ones((CHUNK, CHUNK), dtype=bool))
    pad_lanes = ROUTER_LANES - N_GROUPS - N_EXPERTS
    for i in range(depth):
        proj = in_proj(hb, w_in[i], norm_mix[i])
        ws = jnp.where(tril[None], sgu_w[i], 0.0).astype(BF16)
        bexp = jnp.repeat(jnp.transpose(sgu_b[i]), HEAD_DIM, axis=1)
        cwb = jnp.repeat(cfm_conv_w[i], SUBLANES, axis=0)
        y = mixers(proj, conv3_w[i], sgu_ln_g[i].reshape(1, -1), sgu_ln_b[i].reshape(1, -1), ws, bexp, cwb,
                   cfm_conv_b[i].reshape(1, -1), cfm_ln_g[i].reshape(1, -1), cfm_ln_b[i].reshape(1, -1))
        h = out_proj(y, w_out[i], h)
        wr = jnp.concatenate([router_group_w[i], router_expert_w[i], jnp.zeros((d, pad_lanes), F32)], axis=1)
        br = jnp.concatenate([router_group_b[i], router_expert_b[i], jnp.zeros((pad_lanes,), F32)])
        route, xs, cnt = router(h, norm_moe[i], wr, br.reshape(1, -1))
        wts = route[:, 0:2]
        eid = route[:, 2:4].astype(jnp.int32)
        counts = jnp.sum(cnt[::SUBLANES, N_GROUPS:N_GROUPS + N_EXPERTS], axis=0).astype(jnp.int32)
        tile_expert, tile_first, n_tiles, row_slab, row_gate, pos = _dispatch_tables(eid, wts, counts, moe_tm)
        ys = moe_experts(xs, d, tile_expert, tile_first, n_tiles, row_slab, row_gate,
                         exp_w1[i], exp_w3[i], exp_w2[i], moe_tm)
        h, hn = moe_combine(h, norm_ple[i], ys, pos)
        h, hb = ple_update(hn, ple_gate_w[i], p[i].reshape(t, -1), ple_proj_w[i], h)
    return rmsnorm(h, final_norm, F32).reshape(bsz, seq, d)
```

```python
import functools

import jax
import jax.numpy as jnp
from jax import lax
from jax.experimental import pallas as pl
from jax.experimental.pallas import tpu as pltpu

F32 = jnp.float32
BF16 = jnp.bfloat16
U32 = jnp.uint32
I32 = jnp.int32
EPS = 1e-6

LANES = 128
SUBLANES = 8
HEAD_DIM = 128
CHUNK = 128
N_B_HEADS = 8
A_W = 12 * HEAD_DIM
B_W = N_B_HEADS * HEAD_DIM
C_W = 12 * HEAD_DIM
MIX_W = A_W + B_W + C_W
OFF_AB, OFF_AC, OFF_AX = 0, A_W, 2 * A_W
OFF_BU, OFF_BV = 3 * A_W, 3 * A_W + B_W
OFF_CA, OFF_CG = 3 * A_W + 2 * B_W, 3 * A_W + 2 * B_W + C_W
IN_COLS = 3 * A_W + 2 * B_W + 2 * C_W
CFM_K = 31
CFM_HALO = 32
CONV_STRIDE = 4
CONV_ROWS = SUBLANES * CONV_STRIDE
N_GROUPS = 4
EXPERTS_PER_GROUP = 8
N_EXPERTS = N_GROUPS * EXPERTS_PER_GROUP
EXPERT_BITS = 5
assert N_EXPERTS == 1 << EXPERT_BITS
ROUTER_LANES = LANES
RL_W1, RL_W2, RL_CODE1, RL_CODE2, RL_E1, RL_E2 = 0, 1, 2, 3, 4, 5
VMEM_LIMIT = 56 * 1024 * 1024


def _params(n_axes=1):
    return pltpu.CompilerParams(dimension_semantics=("arbitrary",) * n_axes,
                                vmem_limit_bytes=VMEM_LIMIT)


def _slab_dims(d):
    rows = d // 2 // LANES
    return rows, rows + 4


def _pack_rows(x):
    half = x.shape[1] // 2
    hi = lax.bitcast_convert_type(x[:, :half].astype(BF16).astype(F32), U32)
    lo = lax.bitcast_convert_type(x[:, half:].astype(BF16).astype(F32), U32)
    return hi | (lo >> 16)


def _unpack_rows(w):
    hi = lax.bitcast_convert_type(w & jnp.uint32(0xFFFF0000), F32)
    lo = lax.bitcast_convert_type(w << 16, F32)
    return hi, lo


def _store_slabs(ref, w, m, rows, pitch, meta=None):
    for s in range(rows):
        ref[pl.ds(s, m, stride=pitch), :] = w[:, s * LANES:(s + 1) * LANES]
    first_pad = rows
    if meta is not None:
        ref[pl.ds(rows, m, stride=pitch), :] = meta
        first_pad = rows + 1
    for s in range(first_pad, pitch):
        ref[pl.ds(s, m, stride=pitch), :] = jnp.zeros((m, LANES), w.dtype)


def _load_slabs(ref, lead, row0, m, rows, pitch):
    return jnp.concatenate([ref[lead, pl.ds(row0 + s, m, stride=pitch), :] for s in range(rows)], axis=1)


def _rms_kernel(x_ref, g_ref, o_ref):
    x = x_ref[...]
    ms = jnp.mean(x * x, axis=-1, keepdims=True)
    o_ref[...] = (x * lax.rsqrt(ms + EPS) * g_ref[...]).astype(o_ref.dtype)


def rmsnorm(x, g, out_dtype, tm=256):
    t, d = x.shape
    tm = min(tm, t)
    return pl.pallas_call(
        _rms_kernel, grid=(t // tm,),
        in_specs=[pl.BlockSpec((tm, d), lambda i: (i, 0)), pl.BlockSpec((1, d), lambda i: (0, 0))],
        out_specs=pl.BlockSpec((tm, d), lambda i: (i, 0)),
        out_shape=jax.ShapeDtypeStruct((t, d), out_dtype),
        compiler_params=_params(), name="rmsnorm")(x, g.reshape(1, d))


def _in_proj_kernel(x_ref, w_ref, g_ref, o_ref, wb, rs):
    j, i = pl.program_id(0), pl.program_id(1)

    @pl.when(i == 0)
    def _():
        wb[...] = (w_ref[...] * g_ref[...]).astype(BF16)

    @pl.when(j == 0)
    def _():
        xf = x_ref[...].astype(F32)
        rs[i] = lax.rsqrt(jnp.mean(xf * xf, axis=-1, keepdims=True) + EPS)

    acc = jnp.dot(x_ref[...], wb[...], preferred_element_type=F32)
    o_ref[...] = (acc * rs[i]).astype(o_ref.dtype)


def in_proj(hb, w, g, layer, tm=1024, tn=512):
    t, k = hb.shape
    n = w.shape[2]
    tm, tn = min(tm, t), min(tn, n)
    return pl.pallas_call(
        _in_proj_kernel, grid=(n // tn, t // tm),
        in_specs=[pl.BlockSpec((tm, k), lambda j, i: (i, 0)), pl.BlockSpec((None, k, tn), lambda j, i: (layer, 0, j)),
                  pl.BlockSpec((None, k, 1), lambda j, i: (layer, 0, 0))],
        out_specs=pl.BlockSpec((tm, tn), lambda j, i: (i, j)),
        out_shape=jax.ShapeDtypeStruct((t, n), BF16),
        scratch_shapes=[pltpu.VMEM((k, tn), BF16), pltpu.VMEM((t // tm, tm, 1), F32)],
        compiler_params=_params(2), name="in_proj")(hb, w, g)


def _out_proj_kernel(y_ref, w_ref, r_ref, o_ref, wb):
    @pl.when(pl.program_id(1) == 0)
    def _():
        wb[...] = w_ref[...].astype(BF16)

    o_ref[...] = r_ref[...] + jnp.dot(y_ref[...], wb[...], preferred_element_type=F32)


def out_proj(y, w, res, layer, tm=1024, tn=512):
    t, k = y.shape
    n = w.shape[2]
    tm, tn = min(tm, t), min(tn, n)
    return pl.pallas_call(
        _out_proj_kernel, grid=(n // tn, t // tm),
        in_specs=[pl.BlockSpec((tm, k), lambda j, i: (i, 0)), pl.BlockSpec((None, k, tn), lambda j, i: (layer, 0, j)),
                  pl.BlockSpec((tm, tn), lambda j, i: (i, j))],
        out_specs=pl.BlockSpec((tm, tn), lambda j, i: (i, j)),
        out_shape=jax.ShapeDtypeStruct((t, n), F32),
        scratch_shapes=[pltpu.VMEM((k, tn), BF16)],
        compiler_params=_params(2), name="out_proj")(y, w, res)


def _ple_kernel(x_ref, wg_ref, p_ref, wp_ref, r_ref, o_ref, ob_ref, wgb, wpb):
    @pl.when(pl.program_id(1) == 0)
    def _():
        wgb[...] = wg_ref[...].astype(BF16)
        wpb[...] = wp_ref[...].astype(BF16)

    gate = jax.nn.sigmoid(jnp.dot(x_ref[...], wgb[...], preferred_element_type=F32))
    pp = jnp.dot(p_ref[...].astype(BF16), wpb[...], preferred_element_type=F32)
    h = r_ref[...] + gate * pp
    o_ref[...] = h
    ob_ref[...] = h.astype(BF16)


def ple_update(xn, wg, p, wp, res, layer, tm=1024, tn=512):
    t, k = xn.shape
    n = wg.shape[2]
    r = p.shape[2]
    tm, tn = min(tm, t), min(tn, n)
    return pl.pallas_call(
        _ple_kernel, grid=(n // tn, t // tm),
        in_specs=[pl.BlockSpec((tm, k), lambda j, i: (i, 0)), pl.BlockSpec((None, k, tn), lambda j, i: (layer, 0, j)),
                  pl.BlockSpec((None, tm, r), lambda j, i: (layer, i, 0)),
                  pl.BlockSpec((None, r, tn), lambda j, i: (layer, 0, j)),
                  pl.BlockSpec((tm, tn), lambda j, i: (i, j))],
        out_specs=[pl.BlockSpec((tm, tn), lambda j, i: (i, j)), pl.BlockSpec((tm, tn), lambda j, i: (i, j))],
        out_shape=[jax.ShapeDtypeStruct((t, n), F32), jax.ShapeDtypeStruct((t, n), BF16)],
        scratch_shapes=[pltpu.VMEM((k, tn), BF16), pltpu.VMEM((r, tn), BF16)],
        compiler_params=_params(2), name="ple_update")(xn, wg, p, wp, res)


def _layer_norm(x, g, b):
    mu = jnp.mean(x, axis=-1, keepdims=True)
    xc = x - mu
    var = jnp.mean(xc * xc, axis=-1, keepdims=True)
    return xc * lax.rsqrt(var + EPS) * g + b


def _mixer_kernel(proj_ref, c3w_ref, lng_ref, lnb_ref, ws_ref, bexp_ref, cwb_ref, cb_ref, clg_ref, clb_ref,
                  y_ref, cx_ext, g_s, conv_s):
    tm = y_ref.shape[0]
    n_slab = C_W // LANES

    @pl.when(pl.program_id(0) == 0)
    def _():
        cx_ext[0:SUBLANES, :] = jnp.zeros((SUBLANES, A_W), F32)
        g_s[:, 0:CFM_HALO, :] = jnp.zeros((n_slab, CFM_HALO, LANES), F32)

    a_c = proj_ref[:, OFF_AC:OFF_AC + A_W].astype(F32)
    a_x = proj_ref[:, OFF_AX:OFF_AX + A_W].astype(F32)
    cx_ext[SUBLANES:SUBLANES + tm, :] = a_c * a_x
    conv = c3w_ref[0:1, :] * cx_ext[SUBLANES - 2:SUBLANES - 2 + tm, :]
    conv += c3w_ref[1:2, :] * cx_ext[SUBLANES - 1:SUBLANES - 1 + tm, :]
    conv += c3w_ref[2:3, :] * cx_ext[SUBLANES:SUBLANES + tm, :]
    a_b = proj_ref[:, OFF_AB:OFF_AB + A_W].astype(F32)
    y_ref[:, 0:A_W] = (a_b * conv).astype(y_ref.dtype)
    cx_ext[0:SUBLANES, :] = cx_ext[tm:tm + SUBLANES, :]

    for r0 in range(0, tm, CHUNK):
        u = proj_ref[r0:r0 + CHUNK, OFF_BU:OFF_BU + B_W].astype(F32)
        v = proj_ref[r0:r0 + CHUNK, OFF_BV:OFF_BV + B_W].astype(F32)
        zu = jax.nn.gelu(u)
        zv = _layer_norm(jax.nn.gelu(v), lng_ref[...], lnb_ref[...]).astype(BF16)
        for h in range(N_B_HEADS):
            c0 = h * HEAD_DIM
            s = jnp.dot(ws_ref[h], zv[:, c0:c0 + HEAD_DIM], preferred_element_type=F32)
            s = s + bexp_ref[:, c0:c0 + HEAD_DIM]
            y_ref[r0:r0 + CHUNK, A_W + c0:A_W + c0 + HEAD_DIM] = (zu[:, c0:c0 + HEAD_DIM] * s).astype(y_ref.dtype)

    c_a = proj_ref[:, OFF_CA:OFF_CA + C_W].astype(F32)
    c_g = proj_ref[:, OFF_CG:OFF_CG + C_W].astype(F32)
    glu = c_a * jax.nn.sigmoid(c_g)
    for c in range(n_slab):
        g_s[c, CFM_HALO:CFM_HALO + tm, :] = glu[:, c * LANES:(c + 1) * LANES]
    first_tap_row = CFM_HALO - (CFM_K - 1)

    def conv_rows(rb, carry):
        base = pl.multiple_of(rb * CONV_ROWS, CONV_ROWS)
        for c in range(n_slab):
            accs = [jnp.zeros((SUBLANES, LANES), F32) for _ in range(CONV_STRIDE)]
            for k in range(CFM_K):
                wv = cwb_ref[k * SUBLANES:(k + 1) * SUBLANES, c * LANES:(c + 1) * LANES]
                for j in range(CONV_STRIDE):
                    rows = pl.ds(base + (first_tap_row + k + j), SUBLANES, stride=CONV_STRIDE)
                    accs[j] = accs[j] + wv * g_s[c, rows, :]
            for j in range(CONV_STRIDE):
                conv_s[c, pl.ds(base + j, SUBLANES, stride=CONV_STRIDE), :] = accs[j]
        blk = jnp.concatenate([conv_s[c, pl.ds(base, CONV_ROWS), :] for c in range(n_slab)], axis=1) + cb_ref[...]
        z = _layer_norm(blk, clg_ref[...], clb_ref[...])
        y_ref[pl.ds(base, CONV_ROWS), A_W + B_W:MIX_W] = jax.nn.silu(z).astype(y_ref.dtype)
        return carry

    lax.fori_loop(0, tm // CONV_ROWS, conv_rows, 0)
    g_s[:, 0:CFM_HALO, :] = g_s[:, tm:tm + CFM_HALO, :]


def mixers(proj, layer, stacked, tm=256):
    t = proj.shape[0]
    tm = min(tm, t)
    per_layer = lambda a: pl.BlockSpec((None,) + a.shape[1:], lambda i: (layer,) + (0,) * (a.ndim - 1))
    n_slab = C_W // LANES
    return pl.pallas_call(
        _mixer_kernel, grid=(t // tm,),
        in_specs=[pl.BlockSpec((tm, IN_COLS), lambda i: (i, 0))] + [per_layer(a) for a in stacked],
        out_specs=pl.BlockSpec((tm, MIX_W), lambda i: (i, 0)),
        out_shape=jax.ShapeDtypeStruct((t, MIX_W), BF16),
        scratch_shapes=[pltpu.VMEM((SUBLANES + tm, A_W), F32), pltpu.VMEM((n_slab, CFM_HALO + tm, LANES), F32),
                        pltpu.VMEM((n_slab, tm, LANES), F32)],
        compiler_params=_params(), name="mixers")(proj, *stacked)


def _router_kernel(h_ref, g_ref, w_ref, b_ref, o_ref, xs_ref, cnt_ref, run):
    tm, d = h_ref.shape

    @pl.when(pl.program_id(0) == 0)
    def _():
        run[...] = jnp.zeros(run.shape, F32)

    x = h_ref[...]
    ms = jnp.mean(x * x, axis=-1, keepdims=True)
    hn = x * lax.rsqrt(ms + EPS) * g_ref[...]
    logits = jnp.dot(hn.astype(BF16), w_ref[...].astype(BF16), preferred_element_type=F32) + b_ref[...]
    lane = lax.broadcasted_iota(I32, logits.shape, 1)
    neg = jnp.float32(-jnp.inf)
    is_g = lane < N_GROUPS
    gl = jnp.where(is_g, logits, neg)
    gmax = jnp.max(gl, axis=-1, keepdims=True)
    gidx = jnp.min(jnp.where(gl == gmax, lane, ROUTER_LANES), axis=-1, keepdims=True)
    gsum = jnp.sum(jnp.where(is_g, jnp.exp(gl - gmax), 0.0), axis=-1, keepdims=True)
    g_p = 1.0 / gsum
    e_lane = lane - N_GROUPS
    in_grp = (e_lane >= 0) & (e_lane < N_EXPERTS) & (lax.shift_right_arithmetic(e_lane, 3) == gidx)
    el = jnp.where(in_grp, logits, neg)
    v1 = jnp.max(el, axis=-1, keepdims=True)
    i1 = jnp.min(jnp.where(el == v1, lane, ROUTER_LANES), axis=-1, keepdims=True)
    el2 = jnp.where(lane == i1, neg, el)
    v2 = jnp.max(el2, axis=-1, keepdims=True)
    i2 = jnp.min(jnp.where(el2 == v2, lane, ROUTER_LANES), axis=-1, keepdims=True)
    t2 = jnp.exp(v2 - v1)
    w1 = g_p / (1.0 + t2)
    w2 = w1 * t2
    oh1 = jnp.where(lane == i1, 1.0, 0.0)
    oh2 = jnp.where(lane == i2, 1.0, 0.0)
    r_i = lax.broadcasted_iota(I32, (tm, tm), 0)
    c_i = lax.broadcasted_iota(I32, (tm, tm), 1)
    ltri = jnp.where(r_i > c_i, 1.0, 0.0).astype(BF16)
    before1 = jnp.dot(ltri, oh1.astype(BF16), preferred_element_type=F32)
    before2 = jnp.dot(ltri, oh2.astype(BF16), preferred_element_type=F32)
    tot1 = jnp.sum(oh1, axis=0, keepdims=True)
    tot2 = jnp.sum(oh2, axis=0, keepdims=True)
    seen = run[0:1, :]
    rank1 = jnp.sum(oh1 * (seen + before1), axis=-1, keepdims=True)
    rank2 = jnp.sum(oh2 * (seen + tot1 + before2), axis=-1, keepdims=True)
    seen = seen + tot1 + tot2
    run[...] = jnp.broadcast_to(seen, run.shape)
    cnt_ref[...] = jnp.broadcast_to(seen, cnt_ref.shape)
    e1 = (i1 - N_GROUPS).astype(F32)
    e2 = (i2 - N_GROUPS).astype(F32)
    out = jnp.where(lane == RL_W1, w1, 0.0)
    out = jnp.where(lane == RL_W2, w2, out)
    out = jnp.where(lane == RL_CODE1, rank1 * N_EXPERTS + e1, out)
    out = jnp.where(lane == RL_CODE2, rank2 * N_EXPERTS + e2, out)
    out = jnp.where(lane == RL_E1, e1, out)
    out = jnp.where(lane == RL_E2, e2, out)
    o_ref[...] = out
    rows, pitch = _slab_dims(d)
    _store_slabs(xs_ref, _pack_rows(hn), tm, rows, pitch, meta=lax.bitcast_convert_type(out, U32))


def router(h, g, w, b, layer, tm=256):
    t, d = h.shape
    tm = min(tm, t)
    _, pitch = _slab_dims(d)
    return pl.pallas_call(
        _router_kernel, grid=(t // tm,),
        in_specs=[pl.BlockSpec((tm, d), lambda i: (i, 0)), pl.BlockSpec((None, 1, d), lambda i: (layer, 0, 0)),
                  pl.BlockSpec((None, d, ROUTER_LANES), lambda i: (layer, 0, 0)),
                  pl.BlockSpec((None, 1, ROUTER_LANES), lambda i: (layer, 0, 0))],
        out_specs=[pl.BlockSpec((tm, ROUTER_LANES), lambda i: (i, 0)),
                   pl.BlockSpec((tm * pitch, LANES), lambda i: (i, 0)),
                   pl.BlockSpec((SUBLANES, ROUTER_LANES), lambda i: (0, 0))],
        out_shape=[jax.ShapeDtypeStruct((t, ROUTER_LANES), F32),
                   jax.ShapeDtypeStruct((t * pitch, LANES), U32),
                   jax.ShapeDtypeStruct((SUBLANES, ROUTER_LANES), F32)],
        scratch_shapes=[pltpu.VMEM((SUBLANES, ROUTER_LANES), F32)],
        compiler_params=_params(), name="router")(h, g, w, b)


def _slab_copy(src_hbm, dst_vmem, src_slab, dst_slab, rows, pitch, sem):
    return pltpu.make_async_copy(src_hbm.at[pl.ds(src_slab * pitch, rows), :],
                                 dst_vmem.at[pl.ds(dst_slab * pitch, rows), :], sem)


def _dest_row(code, pad_off_ref):
    return pad_off_ref[code & (N_EXPERTS - 1)] + lax.shift_right_logical(code, EXPERT_BITS)


def _moe_kernel(te_ref, tf_ref, nx_ref, par_ref, tr_ref, nt_ref, po_ref, code_ref,
                xs_hbm, w1_hbm, w3_hbm, w2_hbm, o_ref,
                row_slab, xbuf, ws1, ws3, ws2, w1b, w3b, w2b, xsems, wsems, *, layer):
    d = ws1.shape[1]
    rows, pitch = _slab_dims(d)
    tm = xbuf.shape[1] // pitch
    i = pl.program_id(0)
    n_used = nt_ref[0]
    slot = lax.rem(i, 2)

    def gather(tile, dst_slot, start):
        base = tile * tm

        def body(r, c):
            src = row_slab[base + r] if start else 0
            cp = _slab_copy(xs_hbm, xbuf.at[dst_slot], src, r, rows + 1, pitch, xsems.at[dst_slot])
            cp.start() if start else cp.wait()
            return c

        lax.fori_loop(0, tm, body, 0, unroll=8)

    def weights(expert, wslot, start):
        for src, dst in ((w1_hbm, ws1), (w3_hbm, ws3), (w2_hbm, ws2)):
            cp = pltpu.make_async_copy(src.at[layer, expert], dst.at[wslot], wsems.at[wslot])
            cp.start() if start else cp.wait()

    @pl.when(i == 0)
    def _():
        def clear(r, c):
            row_slab[r] = 0
            return c

        lax.fori_loop(0, row_slab.shape[0], clear, 0, unroll=8)

        def place(q, c):
            row_slab[_dest_row(code_ref[q], po_ref)] = lax.shift_right_logical(q, 1)
            return c

        lax.fori_loop(0, code_ref.shape[0], place, 0, unroll=8)
        weights(te_ref[0], 0, True)
        gather(0, 0, True)

    @pl.when(i < n_used)
    def _():
        expert = te_ref[i]
        gather(i, slot, False)

        @pl.when(i + 1 < n_used)
        def _():
            gather(i + 1, 1 - slot, True)

        @pl.when(tf_ref[i] == 1)
        def _():
            wslot = par_ref[i]
            weights(expert, wslot, False)

            @pl.when(nx_ref[i] >= 0)
            def _():
                weights(nx_ref[i], 1 - wslot, True)

            w1b[...] = ws1[wslot].astype(BF16)
            w3b[...] = ws3[wslot].astype(BF16)
            w2b[...] = ws2[wslot].astype(BF16)

        hi, lo = _unpack_rows(_load_slabs(xbuf, slot, 0, tm, rows, pitch))
        hn = jnp.concatenate([hi.astype(BF16), lo.astype(BF16)], axis=1)
        meta = lax.bitcast_convert_type(xbuf[slot, pl.ds(rows, tm, stride=pitch), :], F32)
        gate = jnp.where(meta[:, RL_E1:RL_E1 + 1] == expert.astype(F32),
                         meta[:, RL_W1:RL_W1 + 1], meta[:, RL_W2:RL_W2 + 1])
        gate = jnp.where(lax.broadcasted_iota(I32, gate.shape, 0) < tr_ref[i], gate, 0.0)
        h1 = jnp.dot(hn, w1b[...], preferred_element_type=F32)
        h3 = jnp.dot(hn, w3b[...], preferred_element_type=F32)
        hid = (jax.nn.silu(h1) * h3 * gate).astype(BF16)
        out = jnp.dot(hid, w2b[...], preferred_element_type=F32)
        _store_slabs(o_ref, _pack_rows(out), tm, rows, pitch)

    @pl.when(i >= n_used)
    def _():
        o_ref[...] = jnp.zeros(o_ref.shape, o_ref.dtype)


def moe_experts(xs, tables, code, w1, w3, w2, layer, tm):
    tile_expert, tile_first, next_expert, tile_parity, tile_rows, n_tiles, pad_off = tables
    nt = tile_expert.shape[0]
    d, ff = w1.shape[2], w1.shape[3]
    _, pitch = _slab_dims(d)
    any_spec = pl.BlockSpec(memory_space=pl.ANY)
    grid_spec = pltpu.PrefetchScalarGridSpec(
        num_scalar_prefetch=8, grid=(nt,),
        in_specs=[any_spec, any_spec, any_spec, any_spec],
        out_specs=pl.BlockSpec((tm * pitch, LANES), lambda i, *_: (i, 0)),
        scratch_shapes=[pltpu.SMEM((nt * tm,), I32),
                        pltpu.VMEM((2, tm * pitch, LANES), U32),
                        pltpu.VMEM((2, d, ff), F32), pltpu.VMEM((2, d, ff), F32), pltpu.VMEM((2, ff, d), F32),
                        pltpu.VMEM((d, ff), BF16), pltpu.VMEM((d, ff), BF16), pltpu.VMEM((ff, d), BF16),
                        pltpu.SemaphoreType.DMA((2,)), pltpu.SemaphoreType.DMA((2,))])
    return pl.pallas_call(
        functools.partial(_moe_kernel, layer=layer), grid_spec=grid_spec,
        out_shape=jax.ShapeDtypeStruct((nt * tm * pitch, LANES), U32),
        compiler_params=_params(), name="moe_experts")(
            tile_expert, tile_first, next_expert, tile_parity, tile_rows, n_tiles, pad_off, code,
            xs, w1, w3, w2)


def _combine_kernel(code_ref, po_ref, h_ref, g_ref, ys_hbm, o_ref, on_ref, buf, sems):
    tm, d = h_ref.shape
    rows, pitch = _slab_dims(d)
    i = pl.program_id(0)
    slot = lax.rem(i, 2)

    def gather(tile, dst_slot, start):
        base = tile * tm

        def body(r, c):
            for k in range(2):
                src = _dest_row(code_ref[2 * (base + r) + k], po_ref) if start else 0
                cp = _slab_copy(ys_hbm, buf.at[dst_slot], src, k * tm + r, rows, pitch, sems.at[dst_slot])
                cp.start() if start else cp.wait()
            return c

        lax.fori_loop(0, tm, body, 0, unroll=4)

    @pl.when(i == 0)
    def _():
        gather(0, 0, True)

    gather(i, slot, False)

    @pl.when(i + 1 < pl.num_programs(0))
    def _():
        gather(i + 1, 1 - slot, True)

    h = h_ref[...]
    for k in range(2):
        hi, lo = _unpack_rows(_load_slabs(buf, slot, k * tm * pitch, tm, rows, pitch))
        h = h + jnp.concatenate([hi, lo], axis=1)
    o_ref[...] = h
    ms = jnp.mean(h * h, axis=-1, keepdims=True)
    on_ref[...] = (h * lax.rsqrt(ms + EPS) * g_ref[...]).astype(on_ref.dtype)


def moe_combine(h, g, ys, code, pad_off, layer, tm=256):
    t, d = h.shape
    tm = min(tm, t)
    _, pitch = _slab_dims(d)
    grid_spec = pltpu.PrefetchScalarGridSpec(
        num_scalar_prefetch=2, grid=(t // tm,),
        in_specs=[pl.BlockSpec((tm, d), lambda i, *_: (i, 0)),
                  pl.BlockSpec((None, 1, d), lambda i, *_: (layer, 0, 0)),
                  pl.BlockSpec(memory_space=pl.ANY)],
        out_specs=[pl.BlockSpec((tm, d), lambda i, *_: (i, 0)), pl.BlockSpec((tm, d), lambda i, *_: (i, 0))],
        scratch_shapes=[pltpu.VMEM((2, 2 * tm * pitch, LANES), U32), pltpu.SemaphoreType.DMA((2,))])
    return pl.pallas_call(
        _combine_kernel, grid_spec=grid_spec,
        out_shape=[jax.ShapeDtypeStruct((t, d), F32), jax.ShapeDtypeStruct((t, d), BF16)],
        compiler_params=_params(), name="moe_combine")(code, pad_off, h, g, ys)


def _dispatch_tables(counts, n_pairs, tm):
    nt = n_pairs // tm + N_EXPERTS
    padded = ((counts + tm - 1) // tm) * tm
    pad_end = jnp.cumsum(padded)
    pad_off = pad_end - padded
    n_tiles = pad_end[-1] // tm
    tile_idx = jnp.arange(nt, dtype=I32)
    tile_expert = jnp.sum((pad_end[None, :] <= (tile_idx * tm)[:, None]).astype(I32), axis=1)
    last_e = jnp.sum((pad_end <= (n_tiles - 1) * tm).astype(I32))
    tile_expert = jnp.where(tile_idx < n_tiles, tile_expert, last_e)
    prev = jnp.concatenate([jnp.full((1,), -1, I32), tile_expert[:-1]])
    tile_first = (tile_expert != prev).astype(I32)
    tile_parity = (jnp.cumsum(tile_first) - 1) & 1
    experts = jnp.arange(N_EXPERTS, dtype=I32)
    later = jnp.where((counts[None, :] > 0) & (experts[None, :] > experts[:, None]), experts[None, :], N_EXPERTS)
    nxt = jnp.min(later, axis=1)
    nxt = jnp.where(nxt < N_EXPERTS, nxt, -1)
    next_expert = nxt[tile_expert]
    tile_rows = jnp.clip(counts[tile_expert] - (tile_idx * tm - pad_off[tile_expert]), 0, tm)
    return (tile_expert, tile_first, next_expert, tile_parity.astype(I32), tile_rows.astype(I32),
            n_tiles.reshape(1).astype(I32), pad_off.astype(I32))


def kernel(x, p, norm_mix, w_in, conv3_w, sgu_ln_g, sgu_ln_b, sgu_w, sgu_b, cfm_conv_w, cfm_conv_b, cfm_ln_g,
           cfm_ln_b, w_out, norm_moe, router_group_w, router_group_b, router_expert_w, router_expert_b,
           exp_w1, exp_w3, exp_w2, norm_ple, ple_gate_w, ple_proj_w, final_norm):
    bsz, seq, d = x.shape
    depth = w_in.shape[0]
    t = bsz * seq
    assert bsz == 1, "conv history is carried across row tiles of a single sequence"
    moe_tm = min(256, t)
    h = x.reshape(t, d)
    hb = h.astype(BF16)
    row = lambda a: a.reshape(depth, 1, -1)
    tril = jnp.tril(jnp.ones((CHUNK, CHUNK), dtype=bool))
    mixer_params = (
        conv3_w, row(sgu_ln_g), row(sgu_ln_b),
        jnp.where(tril[None, None], sgu_w, 0.0).astype(BF16),
        jnp.repeat(jnp.swapaxes(sgu_b, 1, 2), HEAD_DIM, axis=2),
        jnp.repeat(cfm_conv_w, SUBLANES, axis=1),
        row(cfm_conv_b), row(cfm_ln_g), row(cfm_ln_b))
    pad_lanes = ROUTER_LANES - N_GROUPS - N_EXPERTS
    wr = jnp.concatenate([router_group_w, router_expert_w, jnp.zeros((depth, d, pad_lanes), F32)], axis=2)
    br = jnp.concatenate([router_group_b, router_expert_b, jnp.zeros((depth, pad_lanes), F32)], axis=1)
    g_mix, g_moe, g_ple = norm_mix.reshape(depth, d, 1), row(norm_moe), row(norm_ple)
    p3 = p.reshape(depth, t, -1)
    for i in range(depth):
        proj = in_proj(hb, w_in, g_mix, i)
        y = mixers(proj, i, mixer_params)
        h = out_proj(y, w_out, h, i)
        route, xs, cnt = router(h, g_moe, wr, row(br), i)
        code = route[:, RL_CODE1:RL_CODE2 + 1].astype(I32).reshape(-1)
        counts = cnt[0, N_GROUPS:N_GROUPS + N_EXPERTS].astype(I32)
        tables = _dispatch_tables(counts, 2 * t, moe_tm)
        ys = moe_experts(xs, tables, code, exp_w1, exp_w3, exp_w2, i, moe_tm)
        h, hn = moe_combine(h, g_ple, ys, code, tables[-1], i)
        h, hb = ple_update(hn, ple_gate_w, p3, ple_proj_w, h, i)
    return rmsnorm(h, final_norm, F32).reshape(bsz, seq, d)
```

```python
import functools

import jax
import jax.numpy as jnp
from jax import lax
from jax.experimental import pallas as pl
from jax.experimental.pallas import tpu as pltpu

F32 = jnp.float32
BF16 = jnp.bfloat16
U32 = jnp.uint32
I32 = jnp.int32
EPS = 1e-6

LANES = 128
SUBLANES = 8
HEAD_DIM = 128
CHUNK = 128
N_B_HEADS = 8
A_W = 12 * HEAD_DIM
B_W = N_B_HEADS * HEAD_DIM
C_W = 12 * HEAD_DIM
MIX_W = A_W + B_W + C_W
OFF_AB, OFF_AC, OFF_AX = 0, A_W, 2 * A_W
OFF_BU, OFF_BV = 3 * A_W, 3 * A_W + B_W
OFF_CA, OFF_CG = 3 * A_W + 2 * B_W, 3 * A_W + 2 * B_W + C_W
IN_COLS = 3 * A_W + 2 * B_W + 2 * C_W
CFM_K = 31
CFM_HALO = 32
CONV_STRIDE = 4
CONV_ROWS = SUBLANES * CONV_STRIDE
N_GROUPS = 4
EXPERTS_PER_GROUP = 8
N_EXPERTS = N_GROUPS * EXPERTS_PER_GROUP
EXPERT_BITS = 5
assert N_EXPERTS == 1 << EXPERT_BITS
ROUTER_LANES = LANES
RL_W1, RL_W2, RL_CODE1, RL_CODE2, RL_E1, RL_E2 = 0, 1, 2, 3, 4, 5
VMEM_LIMIT = 56 * 1024 * 1024


def _params(n_axes=1):
    return pltpu.CompilerParams(dimension_semantics=("arbitrary",) * n_axes,
                                vmem_limit_bytes=VMEM_LIMIT)


def _slab_dims(d):
    rows = d // 2 // LANES
    return rows, rows + 4


def _pack_pair(hi, lo):
    hi = lax.bitcast_convert_type(hi.astype(BF16).astype(F32), U32)
    lo = lax.bitcast_convert_type(lo.astype(BF16).astype(F32), U32)
    return hi | (lo >> 16)


def _pack_rows(x):
    half = x.shape[1] // 2
    return _pack_pair(x[:, :half], x[:, half:])


def _unpack_rows(w):
    hi = lax.bitcast_convert_type(w & jnp.uint32(0xFFFF0000), F32)
    lo = lax.bitcast_convert_type(w << 16, F32)
    return hi, lo


def _store_slab_rows(ref, w, m, pitch, row0):
    for s in range(w.shape[1] // LANES):
        ref[pl.ds(row0 + s, m, stride=pitch), :] = w[:, s * LANES:(s + 1) * LANES]


def _zero_slab_rows(ref, m, pitch, first, last):
    for s in range(first, last):
        ref[pl.ds(s, m, stride=pitch), :] = jnp.zeros((m, LANES), ref.dtype)


def _store_slabs(ref, w, m, rows, pitch, meta=None):
    _store_slab_rows(ref, w, m, pitch, 0)
    first_pad = rows
    if meta is not None:
        ref[pl.ds(rows, m, stride=pitch), :] = meta
        first_pad = rows + 1
    _zero_slab_rows(ref, m, pitch, first_pad, pitch)


def _load_slabs(ref, lead, row0, m, rows, pitch):
    return jnp.concatenate([ref[lead, pl.ds(row0 + s, m, stride=pitch), :] for s in range(rows)], axis=1)


def _rms_kernel(x_ref, g_ref, o_ref):
    x = x_ref[...]
    ms = jnp.mean(x * x, axis=-1, keepdims=True)
    o_ref[...] = (x * lax.rsqrt(ms + EPS) * g_ref[...]).astype(o_ref.dtype)


def rmsnorm(x, g, out_dtype, tm=256):
    t, d = x.shape
    tm = min(tm, t)
    return pl.pallas_call(
        _rms_kernel, grid=(t // tm,),
        in_specs=[pl.BlockSpec((tm, d), lambda i: (i, 0)), pl.BlockSpec((1, d), lambda i: (0, 0))],
        out_specs=pl.BlockSpec((tm, d), lambda i: (i, 0)),
        out_shape=jax.ShapeDtypeStruct((t, d), out_dtype),
        compiler_params=_params(), name="rmsnorm")(x, g.reshape(1, d))


def _in_proj_kernel(x_ref, w_ref, g_ref, o_ref, wb, rs):
    j, i = pl.program_id(0), pl.program_id(1)

    @pl.when(i == 0)
    def _():
        wb[...] = (w_ref[...] * g_ref[...]).astype(BF16)

    @pl.when(j == 0)
    def _():
        xf = x_ref[...].astype(F32)
        rs[i] = lax.rsqrt(jnp.mean(xf * xf, axis=-1, keepdims=True) + EPS)

    acc = jnp.dot(x_ref[...], wb[...], preferred_element_type=F32)
    o_ref[...] = (acc * rs[i]).astype(o_ref.dtype)


def in_proj(hb, w, g, layer, tm=1024, tn=512):
    t, k = hb.shape
    n = w.shape[2]
    tm, tn = min(tm, t), min(tn, n)
    return pl.pallas_call(
        _in_proj_kernel, grid=(n // tn, t // tm),
        in_specs=[pl.BlockSpec((tm, k), lambda j, i: (i, 0)), pl.BlockSpec((None, k, tn), lambda j, i: (layer, 0, j)),
                  pl.BlockSpec((None, k, 1), lambda j, i: (layer, 0, 0))],
        out_specs=pl.BlockSpec((tm, tn), lambda j, i: (i, j)),
        out_shape=jax.ShapeDtypeStruct((t, n), BF16),
        scratch_shapes=[pltpu.VMEM((k, tn), BF16), pltpu.VMEM((t // tm, tm, 1), F32)],
        compiler_params=_params(2), name="in_proj")(hb, w, g)


def _out_proj_kernel(y_ref, w_ref, r_ref, o_ref, wb):
    @pl.when(pl.program_id(1) == 0)
    def _():
        wb[...] = w_ref[...].astype(BF16)

    o_ref[...] = r_ref[...] + jnp.dot(y_ref[...], wb[...], preferred_element_type=F32)


def out_proj(y, w, res, layer, tm=1024, tn=512):
    t, k = y.shape
    n = w.shape[2]
    tm, tn = min(tm, t), min(tn, n)
    return pl.pallas_call(
        _out_proj_kernel, grid=(n // tn, t // tm),
        in_specs=[pl.BlockSpec((tm, k), lambda j, i: (i, 0)), pl.BlockSpec((None, k, tn), lambda j, i: (layer, 0, j)),
                  pl.BlockSpec((tm, tn), lambda j, i: (i, j))],
        out_specs=pl.BlockSpec((tm, tn), lambda j, i: (i, j)),
        out_shape=jax.ShapeDtypeStruct((t, n), F32),
        scratch_shapes=[pltpu.VMEM((k, tn), BF16)],
        compiler_params=_params(2), name="out_proj")(y, w, res)


def _ple_kernel(x_ref, wg_ref, p_ref, wp_ref, r_ref, o_ref, ob_ref, wgb, wpb):
    @pl.when(pl.program_id(1) == 0)
    def _():
        wgb[...] = wg_ref[...].astype(BF16)
        wpb[...] = wp_ref[...].astype(BF16)

    gate = jax.nn.sigmoid(jnp.dot(x_ref[...], wgb[...], preferred_element_type=F32))
    pp = jnp.dot(p_ref[...].astype(BF16), wpb[...], preferred_element_type=F32)
    h = r_ref[...] + gate * pp
    o_ref[...] = h
    ob_ref[...] = h.astype(BF16)


def ple_update(xn, wg, p, wp, res, layer, tm=1024, tn=512):
    t, k = xn.shape
    n = wg.shape[2]
    r = p.shape[2]
    tm, tn = min(tm, t), min(tn, n)
    return pl.pallas_call(
        _ple_kernel, grid=(n // tn, t // tm),
        in_specs=[pl.BlockSpec((tm, k), lambda j, i: (i, 0)), pl.BlockSpec((None, k, tn), lambda j, i: (layer, 0, j)),
                  pl.BlockSpec((None, tm, r), lambda j, i: (layer, i, 0)),
                  pl.BlockSpec((None, r, tn), lambda j, i: (layer, 0, j)),
                  pl.BlockSpec((tm, tn), lambda j, i: (i, j))],
        out_specs=[pl.BlockSpec((tm, tn), lambda j, i: (i, j)), pl.BlockSpec((tm, tn), lambda j, i: (i, j))],
        out_shape=[jax.ShapeDtypeStruct((t, n), F32), jax.ShapeDtypeStruct((t, n), BF16)],
        scratch_shapes=[pltpu.VMEM((k, tn), BF16), pltpu.VMEM((r, tn), BF16)],
        compiler_params=_params(2), name="ple_update")(xn, wg, p, wp, res)


def _layer_norm(x, g, b):
    mu = jnp.mean(x, axis=-1, keepdims=True)
    xc = x - mu
    var = jnp.mean(xc * xc, axis=-1, keepdims=True)
    return xc * lax.rsqrt(var + EPS) * g + b


def _mixer_kernel(proj_ref, c3w_ref, lng_ref, lnb_ref, ws_ref, bexp_ref, cwb_ref, cb_ref, clg_ref, clb_ref,
                  y_ref, cx_ext, g_s, conv_s):
    tm = y_ref.shape[0]
    n_slab = C_W // LANES

    @pl.when(pl.program_id(0) == 0)
    def _():
        cx_ext[0:SUBLANES, :] = jnp.zeros((SUBLANES, A_W), F32)
        g_s[:, 0:CFM_HALO, :] = jnp.zeros((n_slab, CFM_HALO, LANES), F32)

    a_c = proj_ref[:, OFF_AC:OFF_AC + A_W].astype(F32)
    a_x = proj_ref[:, OFF_AX:OFF_AX + A_W].astype(F32)
    cx_ext[SUBLANES:SUBLANES + tm, :] = a_c * a_x
    conv = c3w_ref[0:1, :] * cx_ext[SUBLANES - 2:SUBLANES - 2 + tm, :]
    conv += c3w_ref[1:2, :] * cx_ext[SUBLANES - 1:SUBLANES - 1 + tm, :]
    conv += c3w_ref[2:3, :] * cx_ext[SUBLANES:SUBLANES + tm, :]
    a_b = proj_ref[:, OFF_AB:OFF_AB + A_W].astype(F32)
    y_ref[:, 0:A_W] = (a_b * conv).astype(y_ref.dtype)
    cx_ext[0:SUBLANES, :] = cx_ext[tm:tm + SUBLANES, :]

    for r0 in range(0, tm, CHUNK):
        u = proj_ref[r0:r0 + CHUNK, OFF_BU:OFF_BU + B_W].astype(F32)
        v = proj_ref[r0:r0 + CHUNK, OFF_BV:OFF_BV + B_W].astype(F32)
        zu = jax.nn.gelu(u)
        zv = _layer_norm(jax.nn.gelu(v), lng_ref[...], lnb_ref[...]).astype(BF16)
        for h in range(N_B_HEADS):
            c0 = h * HEAD_DIM
            s = jnp.dot(ws_ref[h], zv[:, c0:c0 + HEAD_DIM], preferred_element_type=F32)
            s = s + bexp_ref[:, c0:c0 + HEAD_DIM]
            y_ref[r0:r0 + CHUNK, A_W + c0:A_W + c0 + HEAD_DIM] = (zu[:, c0:c0 + HEAD_DIM] * s).astype(y_ref.dtype)

    c_a = proj_ref[:, OFF_CA:OFF_CA + C_W].astype(F32)
    c_g = proj_ref[:, OFF_CG:OFF_CG + C_W].astype(F32)
    glu = c_a * jax.nn.sigmoid(c_g)
    for c in range(n_slab):
        g_s[c, CFM_HALO:CFM_HALO + tm, :] = glu[:, c * LANES:(c + 1) * LANES]
    first_tap_row = CFM_HALO - (CFM_K - 1)

    def conv_rows(rb, carry):
        base = pl.multiple_of(rb * CONV_ROWS, CONV_ROWS)
        for c in range(n_slab):
            accs = [jnp.zeros((SUBLANES, LANES), F32) for _ in range(CONV_STRIDE)]
            for k in range(CFM_K):
                wv = cwb_ref[k * SUBLANES:(k + 1) * SUBLANES, c * LANES:(c + 1) * LANES]
                for j in range(CONV_STRIDE):
                    rows = pl.ds(base + (first_tap_row + k + j), SUBLANES, stride=CONV_STRIDE)
                    accs[j] = accs[j] + wv * g_s[c, rows, :]
            for j in range(CONV_STRIDE):
                conv_s[c, pl.ds(base + j, SUBLANES, stride=CONV_STRIDE), :] = accs[j]
        blk = jnp.concatenate([conv_s[c, pl.ds(base, CONV_ROWS), :] for c in range(n_slab)], axis=1) + cb_ref[...]
        z = _layer_norm(blk, clg_ref[...], clb_ref[...])
        y_ref[pl.ds(base, CONV_ROWS), A_W + B_W:MIX_W] = jax.nn.silu(z).astype(y_ref.dtype)
        return carry

    lax.fori_loop(0, tm // CONV_ROWS, conv_rows, 0)
    g_s[:, 0:CFM_HALO, :] = g_s[:, tm:tm + CFM_HALO, :]


def mixers(proj, layer, stacked, tm=256):
    t = proj.shape[0]
    tm = min(tm, t)
    per_layer = lambda a: pl.BlockSpec((None,) + a.shape[1:], lambda i: (layer,) + (0,) * (a.ndim - 1))
    n_slab = C_W // LANES
    return pl.pallas_call(
        _mixer_kernel, grid=(t // tm,),
        in_specs=[pl.BlockSpec((tm, IN_COLS), lambda i: (i, 0))] + [per_layer(a) for a in stacked],
        out_specs=pl.BlockSpec((tm, MIX_W), lambda i: (i, 0)),
        out_shape=jax.ShapeDtypeStruct((t, MIX_W), BF16),
        scratch_shapes=[pltpu.VMEM((SUBLANES + tm, A_W), F32), pltpu.VMEM((n_slab, CFM_HALO + tm, LANES), F32),
                        pltpu.VMEM((n_slab, tm, LANES), F32)],
        compiler_params=_params(), name="mixers")(proj, *stacked)


def _router_kernel(h_ref, g_ref, w_ref, b_ref, o_ref, xs_ref, cnt_ref, run):
    tm, d = h_ref.shape

    @pl.when(pl.program_id(0) == 0)
    def _():
        run[...] = jnp.zeros(run.shape, F32)

    x = h_ref[...]
    ms = jnp.mean(x * x, axis=-1, keepdims=True)
    hn = x * lax.rsqrt(ms + EPS) * g_ref[...]
    logits = jnp.dot(hn.astype(BF16), w_ref[...].astype(BF16), preferred_element_type=F32) + b_ref[...]
    lane = lax.broadcasted_iota(I32, logits.shape, 1)
    neg = jnp.float32(-jnp.inf)
    is_g = lane < N_GROUPS
    gl = jnp.where(is_g, logits, neg)
    gmax = jnp.max(gl, axis=-1, keepdims=True)
    gidx = jnp.min(jnp.where(gl == gmax, lane, ROUTER_LANES), axis=-1, keepdims=True)
    gsum = jnp.sum(jnp.where(is_g, jnp.exp(gl - gmax), 0.0), axis=-1, keepdims=True)
    g_p = 1.0 / gsum
    e_lane = lane - N_GROUPS
    in_grp = (e_lane >= 0) & (e_lane < N_EXPERTS) & (lax.shift_right_arithmetic(e_lane, 3) == gidx)
    el = jnp.where(in_grp, logits, neg)
    v1 = jnp.max(el, axis=-1, keepdims=True)
    i1 = jnp.min(jnp.where(el == v1, lane, ROUTER_LANES), axis=-1, keepdims=True)
    el2 = jnp.where(lane == i1, neg, el)
    v2 = jnp.max(el2, axis=-1, keepdims=True)
    i2 = jnp.min(jnp.where(el2 == v2, lane, ROUTER_LANES), axis=-1, keepdims=True)
    t2 = jnp.exp(v2 - v1)
    w1 = g_p / (1.0 + t2)
    w2 = w1 * t2
    oh1 = jnp.where(lane == i1, 1.0, 0.0)
    oh2 = jnp.where(lane == i2, 1.0, 0.0)
    r_i = lax.broadcasted_iota(I32, (tm, tm), 0)
    c_i = lax.broadcasted_iota(I32, (tm, tm), 1)
    ltri = jnp.where(r_i > c_i, 1.0, 0.0).astype(BF16)
    before1 = jnp.dot(ltri, oh1.astype(BF16), preferred_element_type=F32)
    before2 = jnp.dot(ltri, oh2.astype(BF16), preferred_element_type=F32)
    tot1 = jnp.sum(oh1, axis=0, keepdims=True)
    tot2 = jnp.sum(oh2, axis=0, keepdims=True)
    seen = run[0:1, :]
    rank1 = jnp.sum(oh1 * (seen + before1), axis=-1, keepdims=True)
    rank2 = jnp.sum(oh2 * (seen + tot1 + before2), axis=-1, keepdims=True)
    seen = seen + tot1 + tot2
    run[...] = jnp.broadcast_to(seen, run.shape)
    cnt_ref[...] = jnp.broadcast_to(seen, cnt_ref.shape)
    e1 = (i1 - N_GROUPS).astype(F32)
    e2 = (i2 - N_GROUPS).astype(F32)
    out = jnp.where(lane == RL_W1, w1, 0.0)
    out = jnp.where(lane == RL_W2, w2, out)
    out = jnp.where(lane == RL_CODE1, rank1 * N_EXPERTS + e1, out)
    out = jnp.where(lane == RL_CODE2, rank2 * N_EXPERTS + e2, out)
    out = jnp.where(lane == RL_E1, e1, out)
    out = jnp.where(lane == RL_E2, e2, out)
    o_ref[...] = out
    rows, pitch = _slab_dims(d)
    _store_slabs(xs_ref, _pack_rows(hn), tm, rows, pitch, meta=lax.bitcast_convert_type(out, U32))


def router(h, g, w, b, layer, tm=256):
    t, d = h.shape
    tm = min(tm, t)
    _, pitch = _slab_dims(d)
    return pl.pallas_call(
        _router_kernel, grid=(t // tm,),
        in_specs=[pl.BlockSpec((tm, d), lambda i: (i, 0)), pl.BlockSpec((None, 1, d), lambda i: (layer, 0, 0)),
                  pl.BlockSpec((None, d, ROUTER_LANES), lambda i: (layer, 0, 0)),
                  pl.BlockSpec((None, 1, ROUTER_LANES), lambda i: (layer, 0, 0))],
        out_specs=[pl.BlockSpec((tm, ROUTER_LANES), lambda i: (i, 0)),
                   pl.BlockSpec((tm * pitch, LANES), lambda i: (i, 0)),
                   pl.BlockSpec((SUBLANES, ROUTER_LANES), lambda i: (0, 0))],
        out_shape=[jax.ShapeDtypeStruct((t, ROUTER_LANES), F32),
                   jax.ShapeDtypeStruct((t * pitch, LANES), U32),
                   jax.ShapeDtypeStruct((SUBLANES, ROUTER_LANES), F32)],
        scratch_shapes=[pltpu.VMEM((SUBLANES, ROUTER_LANES), F32)],
        compiler_params=_params(), name="router")(h, g, w, b)


def _slab_copy(src_hbm, dst_vmem, src_slab, dst_slab, rows, pitch, sem):
    return pltpu.make_async_copy(src_hbm.at[pl.ds(src_slab * pitch, rows), :],
                                 dst_vmem.at[pl.ds(dst_slab * pitch, rows), :], sem)


GATHER_PHASES = 4


def _moe_kernel(te_ref, tf_ref, nx_ref, par_ref, tr_ref, nt_ref, pos_ref,
                xs_hbm, w1_hbm, w3_hbm, w2_hbm, o_ref,
                row_slab, xbuf, ws1, ws3, ws2, w1b, w3b, w2b, xsems, wsems, *, layer):
    d = ws1.shape[1]
    rows, pitch = _slab_dims(d)
    tm = xbuf.shape[1] // pitch
    i = pl.program_id(0)
    n_used = nt_ref[0]
    slot = lax.rem(i, 2)

    def gather(tile, dst_slot, start, first=0, count=tm):
        base = tile * tm

        def body(r, c):
            src = row_slab[base + r] if start else 0
            cp = _slab_copy(xs_hbm, xbuf.at[dst_slot], src, r, rows + 1, pitch, xsems.at[dst_slot])
            cp.start() if start else cp.wait()
            return c

        lax.fori_loop(first, first + count, body, 0, unroll=8)

    def weights(expert, wslot, start):
        for src, dst in ((w1_hbm, ws1), (w3_hbm, ws3), (w2_hbm, ws2)):
            cp = pltpu.make_async_copy(src.at[layer, expert], dst.at[wslot], wsems.at[wslot])
            cp.start(priority=1) if start else cp.wait()

    def prefetch_rows(phase):
        @pl.when(i + 1 < n_used)
        def _():
            per = tm // GATHER_PHASES
            gather(i + 1, 1 - slot, True, phase * per, per)

    @pl.when(i == 0)
    def _():
        def clear(r, c):
            row_slab[r] = 0
            return c

        lax.fori_loop(0, row_slab.shape[0], clear, 0, unroll=16)

        def place(q, c):
            row_slab[pos_ref[q]] = lax.shift_right_logical(q, 1)
            return c

        lax.fori_loop(0, pos_ref.shape[0], place, 0, unroll=16)
        weights(te_ref[0], 0, True)
        gather(0, 0, True)

    @pl.when(i < n_used)
    def _():
        expert = te_ref[i]
        gather(i, slot, False)
        prefetch_rows(0)

        @pl.when(tf_ref[i] == 1)
        def _():
            wslot = par_ref[i]
            weights(expert, wslot, False)

            @pl.when(nx_ref[i] >= 0)
            def _():
                weights(nx_ref[i], 1 - wslot, True)

            w1b[...] = ws1[wslot].astype(BF16)
            w3b[...] = ws3[wslot].astype(BF16)
            w2b[...] = ws2[wslot].astype(BF16)

        hi, lo = _unpack_rows(_load_slabs(xbuf, slot, 0, tm, rows, pitch))
        hn = jnp.concatenate([hi.astype(BF16), lo.astype(BF16)], axis=1)
        meta = lax.bitcast_convert_type(xbuf[slot, pl.ds(rows, tm, stride=pitch), :], F32)
        gate = jnp.where(meta[:, RL_E1:RL_E1 + 1] == expert.astype(F32),
                         meta[:, RL_W1:RL_W1 + 1], meta[:, RL_W2:RL_W2 + 1])
        gate = jnp.where(lax.broadcasted_iota(I32, gate.shape, 0) < tr_ref[i], gate, 0.0)
        h1 = jnp.dot(hn, w1b[...], preferred_element_type=F32)
        prefetch_rows(1)
        h3 = jnp.dot(hn, w3b[...], preferred_element_type=F32)
        hid = (jax.nn.silu(h1) * h3 * gate).astype(BF16)
        prefetch_rows(2)
        half, part = d // 2, d // 4
        for c0 in (0, part):
            o_hi = jnp.dot(hid, w2b[:, c0:c0 + part], preferred_element_type=F32)
            o_lo = jnp.dot(hid, w2b[:, half + c0:half + c0 + part], preferred_element_type=F32)
            _store_slab_rows(o_ref, _pack_pair(o_hi, o_lo), tm, pitch, c0 // LANES)
            if c0 == 0:
                prefetch_rows(3)
        _zero_slab_rows(o_ref, tm, pitch, rows, pitch)

    @pl.when(i >= n_used)
    def _():
        o_ref[...] = jnp.zeros(o_ref.shape, o_ref.dtype)


def moe_experts(xs, tables, pos, w1, w3, w2, layer, tm):
    tile_expert, tile_first, next_expert, tile_parity, tile_rows, n_tiles = tables
    nt = tile_expert.shape[0]
    d, ff = w1.shape[2], w1.shape[3]
    _, pitch = _slab_dims(d)
    any_spec = pl.BlockSpec(memory_space=pl.ANY)
    grid_spec = pltpu.PrefetchScalarGridSpec(
        num_scalar_prefetch=7, grid=(nt,),
        in_specs=[any_spec, any_spec, any_spec, any_spec],
        out_specs=pl.BlockSpec((tm * pitch, LANES), lambda i, *_: (i, 0)),
        scratch_shapes=[pltpu.SMEM((nt * tm,), I32),
                        pltpu.VMEM((2, tm * pitch, LANES), U32),
                        pltpu.VMEM((2, d, ff), F32), pltpu.VMEM((2, d, ff), F32), pltpu.VMEM((2, ff, d), F32),
                        pltpu.VMEM((d, ff), BF16), pltpu.VMEM((d, ff), BF16), pltpu.VMEM((ff, d), BF16),
                        pltpu.SemaphoreType.DMA((2,)), pltpu.SemaphoreType.DMA((2,))])
    return pl.pallas_call(
        functools.partial(_moe_kernel, layer=layer), grid_spec=grid_spec,
        out_shape=jax.ShapeDtypeStruct((nt * tm * pitch, LANES), U32),
        compiler_params=_params(), name="moe_experts")(
            tile_expert, tile_first, next_expert, tile_parity, tile_rows, n_tiles, pos,
            xs, w1, w3, w2)


def _combine_kernel(pos_ref, h_ref, g_ref, ys_hbm, o_ref, on_ref, buf, sems):
    tm, d = h_ref.shape
    rows, pitch = _slab_dims(d)
    i = pl.program_id(0)
    slot = lax.rem(i, 2)

    def gather(tile, dst_slot, start):
        base = tile * tm

        def body(r, c):
            for k in range(2):
                src = pos_ref[2 * (base + r) + k] if start else 0
                cp = _slab_copy(ys_hbm, buf.at[dst_slot], src, k * tm + r, rows, pitch, sems.at[dst_slot])
                cp.start(priority=k) if start else cp.wait()
            return c

        lax.fori_loop(0, tm, body, 0, unroll=4)

    @pl.when(i == 0)
    def _():
        gather(0, 0, True)

    gather(i, slot, False)

    @pl.when(i + 1 < pl.num_programs(0))
    def _():
        gather(i + 1, 1 - slot, True)

    h = h_ref[...]
    for k in range(2):
        hi, lo = _unpack_rows(_load_slabs(buf, slot, k * tm * pitch, tm, rows, pitch))
        h = h + jnp.concatenate([hi, lo], axis=1)
    o_ref[...] = h
    ms = jnp.mean(h * h, axis=-1, keepdims=True)
    on_ref[...] = (h * lax.rsqrt(ms + EPS) * g_ref[...]).astype(on_ref.dtype)


def moe_combine(h, g, ys, pos, layer, tm=256):
    t, d = h.shape
    tm = min(tm, t)
    _, pitch = _slab_dims(d)
    grid_spec = pltpu.PrefetchScalarGridSpec(
        num_scalar_prefetch=1, grid=(t // tm,),
        in_specs=[pl.BlockSpec((tm, d), lambda i, *_: (i, 0)),
                  pl.BlockSpec((None, 1, d), lambda i, *_: (layer, 0, 0)),
                  pl.BlockSpec(memory_space=pl.ANY)],
        out_specs=[pl.BlockSpec((tm, d), lambda i, *_: (i, 0)), pl.BlockSpec((tm, d), lambda i, *_: (i, 0))],
        scratch_shapes=[pltpu.VMEM((2, 2 * tm * pitch, LANES), U32), pltpu.SemaphoreType.DMA((2,))])
    return pl.pallas_call(
        _combine_kernel, grid_spec=grid_spec,
        out_shape=[jax.ShapeDtypeStruct((t, d), F32), jax.ShapeDtypeStruct((t, d), BF16)],
        compiler_params=_params(), name="moe_combine")(pos, h, g, ys)


def _dispatch_tables(counts, n_pairs, tm):
    nt = n_pairs // tm + N_EXPERTS
    padded = ((counts + tm - 1) // tm) * tm
    pad_end = jnp.cumsum(padded)
    pad_off = pad_end - padded
    n_tiles = pad_end[-1] // tm
    tile_idx = jnp.arange(nt, dtype=I32)
    tile_expert = jnp.sum((pad_end[None, :] <= (tile_idx * tm)[:, None]).astype(I32), axis=1)
    last_e = jnp.sum((pad_end <= (n_tiles - 1) * tm).astype(I32))
    tile_expert = jnp.where(tile_idx < n_tiles, tile_expert, last_e)
    prev = jnp.concatenate([jnp.full((1,), -1, I32), tile_expert[:-1]])
    tile_first = (tile_expert != prev).astype(I32)
    tile_parity = (jnp.cumsum(tile_first) - 1) & 1
    experts = jnp.arange(N_EXPERTS, dtype=I32)
    later = jnp.where((counts[None, :] > 0) & (experts[None, :] > experts[:, None]), experts[None, :], N_EXPERTS)
    nxt = jnp.min(later, axis=1)
    nxt = jnp.where(nxt < N_EXPERTS, nxt, -1)
    next_expert = nxt[tile_expert]
    tile_rows = jnp.clip(counts[tile_expert] - (tile_idx * tm - pad_off[tile_expert]), 0, tm)
    tables = (tile_expert, tile_first, next_expert, tile_parity.astype(I32), tile_rows.astype(I32),
              n_tiles.reshape(1).astype(I32))
    return tables, pad_off.astype(I32)


def kernel(x, p, norm_mix, w_in, conv3_w, sgu_ln_g, sgu_ln_b, sgu_w, sgu_b, cfm_conv_w, cfm_conv_b, cfm_ln_g,
           cfm_ln_b, w_out, norm_moe, router_group_w, router_group_b, router_expert_w, router_expert_b,
           exp_w1, exp_w3, exp_w2, norm_ple, ple_gate_w, ple_proj_w, final_norm):
    bsz, seq, d = x.shape
    depth = w_in.shape[0]
    t = bsz * seq
    assert bsz == 1, "conv history is carried across row tiles of a single sequence"
    moe_tm = min(256, t)
    h = x.reshape(t, d)
    hb = h.astype(BF16)
    row = lambda a: a.reshape(depth, 1, -1)
    tril = jnp.tril(jnp.ones((CHUNK, CHUNK), dtype=bool))
    mixer_params = (
        conv3_w, row(sgu_ln_g), row(sgu_ln_b),
        jnp.where(tril[None, None], sgu_w, 0.0).astype(BF16),
        jnp.repeat(jnp.swapaxes(sgu_b, 1, 2), HEAD_DIM, axis=2),
        jnp.repeat(cfm_conv_w, SUBLANES, axis=1),
        row(cfm_conv_b), row(cfm_ln_g), row(cfm_ln_b))
    pad_lanes = ROUTER_LANES - N_GROUPS - N_EXPERTS
    wr = jnp.concatenate([router_group_w, router_expert_w, jnp.zeros((depth, d, pad_lanes), F32)], axis=2)
    br = jnp.concatenate([router_group_b, router_expert_b, jnp.zeros((depth, pad_lanes), F32)], axis=1)
    g_mix, g_moe, g_ple = norm_mix.reshape(depth, d, 1), row(norm_moe), row(norm_ple)
    p3 = p.reshape(depth, t, -1)
    for i in range(depth):
        proj = in_proj(hb, w_in, g_mix, i)
        y = mixers(proj, i, mixer_params)
        h = out_proj(y, w_out, h, i)
        route, xs, cnt = router(h, g_moe, wr, row(br), i)
        code = route[:, RL_CODE1:RL_CODE2 + 1].astype(I32).reshape(-1)
        counts = cnt[0, N_GROUPS:N_GROUPS + N_EXPERTS].astype(I32)
        tables, pad_off = _dispatch_tables(counts, 2 * t, moe_tm)
        pos = pad_off[code & (N_EXPERTS - 1)] + lax.shift_right_logical(code, EXPERT_BITS)
        ys = moe_experts(xs, tables, pos, exp_w1, exp_w3, exp_w2, i, moe_tm)
        h, hn = moe_combine(h, g_ple, ys, pos, i)
        h, hb = ple_update(hn, ple_gate_w, p3, ple_proj_w, h, i)
    return rmsnorm(h, final_norm, F32).reshape(bsz, seq, d)
```

```python
import functools

import jax
import jax.numpy as jnp
from jax import lax
from jax.experimental import pallas as pl
from jax.experimental.pallas import tpu as pltpu

F32 = jnp.float32
BF16 = jnp.bfloat16
U32 = jnp.uint32
I32 = jnp.int32
EPS = 1e-6

LANES = 128
SUBLANES = 8
HEAD_DIM = 128
CHUNK = 128
N_B_HEADS = 8
A_W = 12 * HEAD_DIM
B_W = N_B_HEADS * HEAD_DIM
C_W = 12 * HEAD_DIM
MIX_W = A_W + B_W + C_W
OFF_AB, OFF_AC, OFF_AX = 0, A_W, 2 * A_W
OFF_BU, OFF_BV = 3 * A_W, 3 * A_W + B_W
OFF_CA, OFF_CG = 3 * A_W + 2 * B_W, 3 * A_W + 2 * B_W + C_W
IN_COLS = 3 * A_W + 2 * B_W + 2 * C_W
CFM_K = 31
CFM_HALO = 32
CONV_STRIDE = 4
CONV_ROWS = SUBLANES * CONV_STRIDE
N_GROUPS = 4
EXPERTS_PER_GROUP = 8
N_EXPERTS = N_GROUPS * EXPERTS_PER_GROUP
EXPERT_BITS = 5
assert N_EXPERTS == 1 << EXPERT_BITS
ROUTER_LANES = LANES
RL_W1, RL_W2, RL_CODE1, RL_CODE2, RL_E1, RL_E2 = 0, 1, 2, 3, 4, 5
VMEM_LIMIT = 56 * 1024 * 1024


def _params(n_axes=1):
    return pltpu.CompilerParams(dimension_semantics=("arbitrary",) * n_axes,
                                vmem_limit_bytes=VMEM_LIMIT)


def _slab_dims(d):
    rows = d // 2 // LANES
    return rows, rows + 4


def _pack_pair(hi, lo):
    hi = lax.bitcast_convert_type(hi.astype(BF16).astype(F32), U32)
    lo = lax.bitcast_convert_type(lo.astype(BF16).astype(F32), U32)
    return hi | (lo >> 16)


def _pack_rows(x):
    half = x.shape[1] // 2
    return _pack_pair(x[:, :half], x[:, half:])


def _unpack_rows(w):
    hi = lax.bitcast_convert_type(w & jnp.uint32(0xFFFF0000), F32)
    lo = lax.bitcast_convert_type(w << 16, F32)
    return hi, lo


def _store_slab_rows(ref, w, m, pitch, row0):
    for s in range(w.shape[1] // LANES):
        ref[pl.ds(row0 + s, m, stride=pitch), :] = w[:, s * LANES:(s + 1) * LANES]


def _zero_slab_rows(ref, m, pitch, first, last):
    for s in range(first, last):
        ref[pl.ds(s, m, stride=pitch), :] = jnp.zeros((m, LANES), ref.dtype)


def _store_slabs(ref, w, m, rows, pitch, meta=None):
    _store_slab_rows(ref, w, m, pitch, 0)
    first_pad = rows
    if meta is not None:
        ref[pl.ds(rows, m, stride=pitch), :] = meta
        first_pad = rows + 1
    _zero_slab_rows(ref, m, pitch, first_pad, pitch)


def _load_slabs(ref, lead, row0, m, rows, pitch):
    return jnp.concatenate([ref[lead, pl.ds(row0 + s, m, stride=pitch), :] for s in range(rows)], axis=1)


def _rms_kernel(x_ref, g_ref, o_ref):
    x = x_ref[...]
    ms = jnp.mean(x * x, axis=-1, keepdims=True)
    o_ref[...] = (x * lax.rsqrt(ms + EPS) * g_ref[...]).astype(o_ref.dtype)


def rmsnorm(x, g, out_dtype, tm=256):
    t, d = x.shape
    tm = min(tm, t)
    return pl.pallas_call(
        _rms_kernel, grid=(t // tm,),
        in_specs=[pl.BlockSpec((tm, d), lambda i: (i, 0)), pl.BlockSpec((1, d), lambda i: (0, 0))],
        out_specs=pl.BlockSpec((tm, d), lambda i: (i, 0)),
        out_shape=jax.ShapeDtypeStruct((t, d), out_dtype),
        compiler_params=_params(), name="rmsnorm")(x, g.reshape(1, d))


def _in_proj_kernel(x_ref, w_ref, g_ref, o_ref, wb, rs):
    j, i = pl.program_id(0), pl.program_id(1)

    @pl.when(i == 0)
    def _():
        wb[...] = (w_ref[...] * g_ref[...]).astype(BF16)

    @pl.when(j == 0)
    def _():
        xf = x_ref[...].astype(F32)
        rs[i] = lax.rsqrt(jnp.mean(xf * xf, axis=-1, keepdims=True) + EPS)

    acc = jnp.dot(x_ref[...], wb[...], preferred_element_type=F32)
    o_ref[...] = (acc * rs[i]).astype(o_ref.dtype)


def in_proj(hb, w, g, layer, tm=1024, tn=512):
    t, k = hb.shape
    n = w.shape[2]
    tm, tn = min(tm, t), min(tn, n)
    return pl.pallas_call(
        _in_proj_kernel, grid=(n // tn, t // tm),
        in_specs=[pl.BlockSpec((tm, k), lambda j, i: (i, 0)), pl.BlockSpec((None, k, tn), lambda j, i: (layer, 0, j)),
                  pl.BlockSpec((None, k, 1), lambda j, i: (layer, 0, 0))],
        out_specs=pl.BlockSpec((tm, tn), lambda j, i: (i, j)),
        out_shape=jax.ShapeDtypeStruct((t, n), BF16),
        scratch_shapes=[pltpu.VMEM((k, tn), BF16), pltpu.VMEM((t // tm, tm, 1), F32)],
        compiler_params=_params(2), name="in_proj")(hb, w, g)


def _out_proj_kernel(y_ref, w_ref, r_ref, o_ref, wb):
    @pl.when(pl.program_id(1) == 0)
    def _():
        wb[...] = w_ref[...].astype(BF16)

    o_ref[...] = r_ref[...] + jnp.dot(y_ref[...], wb[...], preferred_element_type=F32)


def out_proj(y, w, res, layer, tm=1024, tn=512):
    t, k = y.shape
    n = w.shape[2]
    tm, tn = min(tm, t), min(tn, n)
    return pl.pallas_call(
        _out_proj_kernel, grid=(n // tn, t // tm),
        in_specs=[pl.BlockSpec((tm, k), lambda j, i: (i, 0)), pl.BlockSpec((None, k, tn), lambda j, i: (layer, 0, j)),
                  pl.BlockSpec((tm, tn), lambda j, i: (i, j))],
        out_specs=pl.BlockSpec((tm, tn), lambda j, i: (i, j)),
        out_shape=jax.ShapeDtypeStruct((t, n), F32),
        scratch_shapes=[pltpu.VMEM((k, tn), BF16)],
        compiler_params=_params(2), name="out_proj")(y, w, res)


def _ple_kernel(x_ref, wg_ref, p_ref, wp_ref, r_ref, o_ref, ob_ref, wgb, wpb):
    @pl.when(pl.program_id(1) == 0)
    def _():
        wgb[...] = wg_ref[...].astype(BF16)
        wpb[...] = wp_ref[...].astype(BF16)

    gate = jax.nn.sigmoid(jnp.dot(x_ref[...], wgb[...], preferred_element_type=F32))
    pp = jnp.dot(p_ref[...].astype(BF16), wpb[...], preferred_element_type=F32)
    h = r_ref[...] + gate * pp
    o_ref[...] = h
    ob_ref[...] = h.astype(BF16)


def ple_update(xn, wg, p, wp, res, layer, tm=1024, tn=512):
    t, k = xn.shape
    n = wg.shape[2]
    r = p.shape[2]
    tm, tn = min(tm, t), min(tn, n)
    return pl.pallas_call(
        _ple_kernel, grid=(n // tn, t // tm),
        in_specs=[pl.BlockSpec((tm, k), lambda j, i: (i, 0)), pl.BlockSpec((None, k, tn), lambda j, i: (layer, 0, j)),
                  pl.BlockSpec((None, tm, r), lambda j, i: (layer, i, 0)),
                  pl.BlockSpec((None, r, tn), lambda j, i: (layer, 0, j)),
                  pl.BlockSpec((tm, tn), lambda j, i: (i, j))],
        out_specs=[pl.BlockSpec((tm, tn), lambda j, i: (i, j)), pl.BlockSpec((tm, tn), lambda j, i: (i, j))],
        out_shape=[jax.ShapeDtypeStruct((t, n), F32), jax.ShapeDtypeStruct((t, n), BF16)],
        scratch_shapes=[pltpu.VMEM((k, tn), BF16), pltpu.VMEM((r, tn), BF16)],
        compiler_params=_params(2), name="ple_update")(xn, wg, p, wp, res)


def _layer_norm(x, g, b):
    mu = jnp.mean(x, axis=-1, keepdims=True)
    xc = x - mu
    var = jnp.mean(xc * xc, axis=-1, keepdims=True)
    return xc * lax.rsqrt(var + EPS) * g + b


def _mixer_kernel(proj_ref, c3w_ref, lng_ref, lnb_ref, ws_ref, bexp_ref, cwb_ref, cb_ref, clg_ref, clb_ref,
                  y_ref, cx_ext, g_s, conv_s):
    tm = y_ref.shape[0]
    n_slab = C_W // LANES

    @pl.when(pl.program_id(0) == 0)
    def _():
        cx_ext[0:SUBLANES, :] = jnp.zeros((SUBLANES, A_W), F32)
        g_s[:, 0:CFM_HALO, :] = jnp.zeros((n_slab, CFM_HALO, LANES), F32)

    a_c = proj_ref[:, OFF_AC:OFF_AC + A_W].astype(F32)
    a_x = proj_ref[:, OFF_AX:OFF_AX + A_W].astype(F32)
    cx_ext[SUBLANES:SUBLANES + tm, :] = a_c * a_x
    conv = c3w_ref[0:1, :] * cx_ext[SUBLANES - 2:SUBLANES - 2 + tm, :]
    conv += c3w_ref[1:2, :] * cx_ext[SUBLANES - 1:SUBLANES - 1 + tm, :]
    conv += c3w_ref[2:3, :] * cx_ext[SUBLANES:SUBLANES + tm, :]
    a_b = proj_ref[:, OFF_AB:OFF_AB + A_W].astype(F32)
    y_ref[:, 0:A_W] = (a_b * conv).astype(y_ref.dtype)
    cx_ext[0:SUBLANES, :] = cx_ext[tm:tm + SUBLANES, :]

    for r0 in range(0, tm, CHUNK):
        u = proj_ref[r0:r0 + CHUNK, OFF_BU:OFF_BU + B_W].astype(F32)
        v = proj_ref[r0:r0 + CHUNK, OFF_BV:OFF_BV + B_W].astype(F32)
        zu = jax.nn.gelu(u)
        zv = _layer_norm(jax.nn.gelu(v), lng_ref[...], lnb_ref[...]).astype(BF16)
        for h in range(N_B_HEADS):
            c0 = h * HEAD_DIM
            s = jnp.dot(ws_ref[h], zv[:, c0:c0 + HEAD_DIM], preferred_element_type=F32)
            s = s + bexp_ref[:, c0:c0 + HEAD_DIM]
            y_ref[r0:r0 + CHUNK, A_W + c0:A_W + c0 + HEAD_DIM] = (zu[:, c0:c0 + HEAD_DIM] * s).astype(y_ref.dtype)

    c_a = proj_ref[:, OFF_CA:OFF_CA + C_W].astype(F32)
    c_g = proj_ref[:, OFF_CG:OFF_CG + C_W].astype(F32)
    glu = c_a * jax.nn.sigmoid(c_g)
    for c in range(n_slab):
        g_s[c, CFM_HALO:CFM_HALO + tm, :] = glu[:, c * LANES:(c + 1) * LANES]
    first_tap_row = CFM_HALO - (CFM_K - 1)

    def conv_rows(rb, carry):
        base = pl.multiple_of(rb * CONV_ROWS, CONV_ROWS)
        for c in range(n_slab):
            accs = [jnp.zeros((SUBLANES, LANES), F32) for _ in range(CONV_STRIDE)]
            for k in range(CFM_K):
                wv = cwb_ref[k * SUBLANES:(k + 1) * SUBLANES, c * LANES:(c + 1) * LANES]
                for j in range(CONV_STRIDE):
                    rows = pl.ds(base + (first_tap_row + k + j), SUBLANES, stride=CONV_STRIDE)
                    accs[j] = accs[j] + wv * g_s[c, rows, :]
            for j in range(CONV_STRIDE):
                conv_s[c, pl.ds(base + j, SUBLANES, stride=CONV_STRIDE), :] = accs[j]
        blk = jnp.concatenate([conv_s[c, pl.ds(base, CONV_ROWS), :] for c in range(n_slab)], axis=1) + cb_ref[...]
        z = _layer_norm(blk, clg_ref[...], clb_ref[...])
        y_ref[pl.ds(base, CONV_ROWS), A_W + B_W:MIX_W] = jax.nn.silu(z).astype(y_ref.dtype)
        return carry

    lax.fori_loop(0, tm // CONV_ROWS, conv_rows, 0)
    g_s[:, 0:CFM_HALO, :] = g_s[:, tm:tm + CFM_HALO, :]


def mixers(proj, layer, stacked, tm=256):
    t = proj.shape[0]
    tm = min(tm, t)
    per_layer = lambda a: pl.BlockSpec((None,) + a.shape[1:], lambda i: (layer,) + (0,) * (a.ndim - 1))
    n_slab = C_W // LANES
    return pl.pallas_call(
        _mixer_kernel, grid=(t // tm,),
        in_specs=[pl.BlockSpec((tm, IN_COLS), lambda i: (i, 0))] + [per_layer(a) for a in stacked],
        out_specs=pl.BlockSpec((tm, MIX_W), lambda i: (i, 0)),
        out_shape=jax.ShapeDtypeStruct((t, MIX_W), BF16),
        scratch_shapes=[pltpu.VMEM((SUBLANES + tm, A_W), F32), pltpu.VMEM((n_slab, CFM_HALO + tm, LANES), F32),
                        pltpu.VMEM((n_slab, tm, LANES), F32)],
        compiler_params=_params(), name="mixers")(proj, *stacked)


def _router_kernel(h_ref, g_ref, w_ref, b_ref, o_ref, xs_ref, cnt_ref, run):
    tm, d = h_ref.shape

    @pl.when(pl.program_id(0) == 0)
    def _():
        run[...] = jnp.zeros(run.shape, F32)

    x = h_ref[...]
    ms = jnp.mean(x * x, axis=-1, keepdims=True)
    hn = x * lax.rsqrt(ms + EPS) * g_ref[...]
    logits = jnp.dot(hn.astype(BF16), w_ref[...].astype(BF16), preferred_element_type=F32) + b_ref[...]
    lane = lax.broadcasted_iota(I32, logits.shape, 1)
    neg = jnp.float32(-jnp.inf)
    is_g = lane < N_GROUPS
    gl = jnp.where(is_g, logits, neg)
    gmax = jnp.max(gl, axis=-1, keepdims=True)
    gidx = jnp.min(jnp.where(gl == gmax, lane, ROUTER_LANES), axis=-1, keepdims=True)
    gsum = jnp.sum(jnp.where(is_g, jnp.exp(gl - gmax), 0.0), axis=-1, keepdims=True)
    g_p = 1.0 / gsum
    e_lane = lane - N_GROUPS
    in_grp = (e_lane >= 0) & (e_lane < N_EXPERTS) & (lax.shift_right_arithmetic(e_lane, 3) == gidx)
    el = jnp.where(in_grp, logits, neg)
    v1 = jnp.max(el, axis=-1, keepdims=True)
    i1 = jnp.min(jnp.where(el == v1, lane, ROUTER_LANES), axis=-1, keepdims=True)
    el2 = jnp.where(lane == i1, neg, el)
    v2 = jnp.max(el2, axis=-1, keepdims=True)
    i2 = jnp.min(jnp.where(el2 == v2, lane, ROUTER_LANES), axis=-1, keepdims=True)
    t2 = jnp.exp(v2 - v1)
    w1 = g_p / (1.0 + t2)
    w2 = w1 * t2
    oh1 = jnp.where(lane == i1, 1.0, 0.0)
    oh2 = jnp.where(lane == i2, 1.0, 0.0)
    r_i = lax.broadcasted_iota(I32, (tm, tm), 0)
    c_i = lax.broadcasted_iota(I32, (tm, tm), 1)
    ltri = jnp.where(r_i > c_i, 1.0, 0.0).astype(BF16)
    before1 = jnp.dot(ltri, oh1.astype(BF16), preferred_element_type=F32)
    before2 = jnp.dot(ltri, oh2.astype(BF16), preferred_element_type=F32)
    tot1 = jnp.sum(oh1, axis=0, keepdims=True)
    tot2 = jnp.sum(oh2, axis=0, keepdims=True)
    seen = run[0:1, :]
    rank1 = jnp.sum(oh1 * (seen + before1), axis=-1, keepdims=True)
    rank2 = jnp.sum(oh2 * (seen + tot1 + before2), axis=-1, keepdims=True)
    seen = seen + tot1 + tot2
    run[...] = jnp.broadcast_to(seen, run.shape)
    cnt_ref[...] = jnp.broadcast_to(seen, cnt_ref.shape)
    e1 = (i1 - N_GROUPS).astype(F32)
    e2 = (i2 - N_GROUPS).astype(F32)
    out = jnp.where(lane == RL_W1, w1, 0.0)
    out = jnp.where(lane == RL_W2, w2, out)
    out = jnp.where(lane == RL_CODE1, rank1 * N_EXPERTS + e1, out)
    out = jnp.where(lane == RL_CODE2, rank2 * N_EXPERTS + e2, out)
    out = jnp.where(lane == RL_E1, e1, out)
    out = jnp.where(lane == RL_E2, e2, out)
    o_ref[...] = out
    rows, pitch = _slab_dims(d)
    _store_slabs(xs_ref, _pack_rows(hn), tm, rows, pitch, meta=lax.bitcast_convert_type(out, U32))


def router(h, g, w, b, layer, tm=256):
    t, d = h.shape
    tm = min(tm, t)
    _, pitch = _slab_dims(d)
    return pl.pallas_call(
        _router_kernel, grid=(t // tm,),
        in_specs=[pl.BlockSpec((tm, d), lambda i: (i, 0)), pl.BlockSpec((None, 1, d), lambda i: (layer, 0, 0)),
                  pl.BlockSpec((None, d, ROUTER_LANES), lambda i: (layer, 0, 0)),
                  pl.BlockSpec((None, 1, ROUTER_LANES), lambda i: (layer, 0, 0))],
        out_specs=[pl.BlockSpec((tm, ROUTER_LANES), lambda i: (i, 0)),
                   pl.BlockSpec((tm * pitch, LANES), lambda i: (i, 0)),
                   pl.BlockSpec((SUBLANES, ROUTER_LANES), lambda i: (0, 0))],
        out_shape=[jax.ShapeDtypeStruct((t, ROUTER_LANES), F32),
                   jax.ShapeDtypeStruct((t * pitch, LANES), U32),
                   jax.ShapeDtypeStruct((SUBLANES, ROUTER_LANES), F32)],
        scratch_shapes=[pltpu.VMEM((SUBLANES, ROUTER_LANES), F32)],
        compiler_params=_params(), name="router")(h, g, w, b)


def _slab_copy(src_hbm, dst_vmem, src_slab, dst_slab, rows, pitch, sem):
    return pltpu.make_async_copy(src_hbm.at[pl.ds(src_slab * pitch, rows), :],
                                 dst_vmem.at[pl.ds(dst_slab * pitch, rows), :], sem)


def _moe_kernel(te_ref, tf_ref, nx_ref, par_ref, tr_ref, nt_ref, pos_ref,
                xs_hbm, w1_hbm, w3_hbm, w2_hbm, o_ref,
                row_slab, xbuf, ws1, ws3, ws2, w1b, w3b, w2b, xsems, wsems, *, layer):
    d = ws1.shape[1]
    rows, pitch = _slab_dims(d)
    tm = xbuf.shape[1] // pitch
    i = pl.program_id(0)
    n_used = nt_ref[0]
    slot = lax.rem(i, 2)

    def start_gather(tile, dst_slot):
        base = tile * tm

        def body(r, c):
            _slab_copy(xs_hbm, xbuf.at[dst_slot], row_slab[base + r], r, rows + 1, pitch,
                       xsems.at[dst_slot]).start()
            return c

        lax.fori_loop(0, tm, body, 0, unroll=8)

    def wait_gather(dst_slot):
        n = tm * (rows + 1)
        pltpu.make_async_copy(xs_hbm.at[pl.ds(0, n), :], xbuf.at[dst_slot].at[pl.ds(0, n), :],
                              xsems.at[dst_slot]).wait()

    def weights(expert, wslot, start):
        for src, dst in ((w1_hbm, ws1), (w3_hbm, ws3), (w2_hbm, ws2)):
            cp = pltpu.make_async_copy(src.at[layer, expert], dst.at[wslot], wsems.at[wslot])
            cp.start() if start else cp.wait()

    @pl.when(i == 0)
    def _():
        def clear(r, c):
            row_slab[r] = 0
            return c

        lax.fori_loop(0, row_slab.shape[0], clear, 0, unroll=16)

        def place(q, c):
            row_slab[pos_ref[q]] = lax.shift_right_logical(q, 1)
            return c

        lax.fori_loop(0, pos_ref.shape[0], place, 0, unroll=16)
        weights(te_ref[0], 0, True)
        start_gather(0, 0)

    @pl.when(i < n_used)
    def _():
        expert = te_ref[i]
        wait_gather(slot)

        @pl.when(i + 1 < n_used)
        def _():
            start_gather(i + 1, 1 - slot)

        @pl.when(tf_ref[i] == 1)
        def _():
            wslot = par_ref[i]
            weights(expert, wslot, False)

            @pl.when(nx_ref[i] >= 0)
            def _():
                weights(nx_ref[i], 1 - wslot, True)

            w1b[...] = ws1[wslot].astype(BF16)
            w3b[...] = ws3[wslot].astype(BF16)
            w2b[...] = ws2[wslot].astype(BF16)

        hi, lo = _unpack_rows(_load_slabs(xbuf, slot, 0, tm, rows, pitch))
        hn = jnp.concatenate([hi.astype(BF16), lo.astype(BF16)], axis=1)
        meta = lax.bitcast_convert_type(xbuf[slot, pl.ds(rows, tm, stride=pitch), :], F32)
        gate = jnp.where(meta[:, RL_E1:RL_E1 + 1] == expert.astype(F32),
                         meta[:, RL_W1:RL_W1 + 1], meta[:, RL_W2:RL_W2 + 1])
        gate = jnp.where(lax.broadcasted_iota(I32, gate.shape, 0) < tr_ref[i], gate, 0.0)
        h1 = jnp.dot(hn, w1b[...], preferred_element_type=F32)
        h3 = jnp.dot(hn, w3b[...], preferred_element_type=F32)
        hid = (jax.nn.silu(h1) * h3 * gate).astype(BF16)
        out = jnp.dot(hid, w2b[...], preferred_element_type=F32)
        _store_slabs(o_ref, _pack_rows(out), tm, rows, pitch)

    @pl.when(i >= n_used)
    def _():
        o_ref[...] = jnp.zeros(o_ref.shape, o_ref.dtype)


def moe_experts(xs, tables, pos, w1, w3, w2, layer, tm):
    tile_expert, tile_first, next_expert, tile_parity, tile_rows, n_tiles = tables
    nt = tile_expert.shape[0]
    d, ff = w1.shape[2], w1.shape[3]
    _, pitch = _slab_dims(d)
    any_spec = pl.BlockSpec(memory_space=pl.ANY)
    grid_spec = pltpu.PrefetchScalarGridSpec(
        num_scalar_prefetch=7, grid=(nt,),
        in_specs=[any_spec, any_spec, any_spec, any_spec],
        out_specs=pl.BlockSpec((tm * pitch, LANES), lambda i, *_: (i, 0)),
        scratch_shapes=[pltpu.SMEM((nt * tm,), I32),
                        pltpu.VMEM((2, tm * pitch, LANES), U32),
                        pltpu.VMEM((2, d, ff), F32), pltpu.VMEM((2, d, ff), F32), pltpu.VMEM((2, ff, d), F32),
                        pltpu.VMEM((d, ff), BF16), pltpu.VMEM((d, ff), BF16), pltpu.VMEM((ff, d), BF16),
                        pltpu.SemaphoreType.DMA((2,)), pltpu.SemaphoreType.DMA((2,))])
    return pl.pallas_call(
        functools.partial(_moe_kernel, layer=layer), grid_spec=grid_spec,
        out_shape=jax.ShapeDtypeStruct((nt * tm * pitch, LANES), U32),
        compiler_params=_params(), name="moe_experts")(
            tile_expert, tile_first, next_expert, tile_parity, tile_rows, n_tiles, pos,
            xs, w1, w3, w2)


def _combine_kernel(pos_ref, h_ref, g_ref, ys_hbm, o_ref, on_ref, buf, sems):
    tm, d = h_ref.shape
    rows, pitch = _slab_dims(d)
    i = pl.program_id(0)
    slot = lax.rem(i, 2)

    def start_gather(tile, dst_slot):
        base = tile * tm

        def body(r, c):
            for k in range(2):
                _slab_copy(ys_hbm, buf.at[dst_slot], pos_ref[2 * (base + r) + k], k * tm + r, rows, pitch,
                           sems.at[dst_slot]).start()
            return c

        lax.fori_loop(0, tm, body, 0, unroll=4)

    def wait_gather(dst_slot):
        n = 2 * tm * rows
        pltpu.make_async_copy(ys_hbm.at[pl.ds(0, n), :], buf.at[dst_slot].at[pl.ds(0, n), :],
                              sems.at[dst_slot]).wait()

    @pl.when(i == 0)
    def _():
        start_gather(0, 0)

    wait_gather(slot)

    @pl.when(i + 1 < pl.num_programs(0))
    def _():
        start_gather(i + 1, 1 - slot)

    h = h_ref[...]
    for k in range(2):
        hi, lo = _unpack_rows(_load_slabs(buf, slot, k * tm * pitch, tm, rows, pitch))
        h = h + jnp.concatenate([hi, lo], axis=1)
    o_ref[...] = h
    ms = jnp.mean(h * h, axis=-1, keepdims=True)
    on_ref[...] = (h * lax.rsqrt(ms + EPS) * g_ref[...]).astype(on_ref.dtype)


def moe_combine(h, g, ys, pos, layer, tm=256):
    t, d = h.shape
    tm = min(tm, t)
    _, pitch = _slab_dims(d)
    grid_spec = pltpu.PrefetchScalarGridSpec(
        num_scalar_prefetch=1, grid=(t // tm,),
        in_specs=[pl.BlockSpec((tm, d), lambda i, *_: (i, 0)),
                  pl.BlockSpec((None, 1, d), lambda i, *_: (layer, 0, 0)),
                  pl.BlockSpec(memory_space=pl.ANY)],
        out_specs=[pl.BlockSpec((tm, d), lambda i, *_: (i, 0)), pl.BlockSpec((tm, d), lambda i, *_: (i, 0))],
        scratch_shapes=[pltpu.VMEM((2, 2 * tm * pitch, LANES), U32), pltpu.SemaphoreType.DMA((2,))])
    return pl.pallas_call(
        _combine_kernel, grid_spec=grid_spec,
        out_shape=[jax.ShapeDtypeStruct((t, d), F32), jax.ShapeDtypeStruct((t, d), BF16)],
        compiler_params=_params(), name="moe_combine")(pos, h, g, ys)


def _dispatch_tables(counts, n_pairs, tm):
    nt = n_pairs // tm + N_EXPERTS
    padded = ((counts + tm - 1) // tm) * tm
    pad_end = jnp.cumsum(padded)
    pad_off = pad_end - padded
    n_tiles = pad_end[-1] // tm
    tile_idx = jnp.arange(nt, dtype=I32)
    tile_expert = jnp.sum((pad_end[None, :] <= (tile_idx * tm)[:, None]).astype(I32), axis=1)
    last_e = jnp.sum((pad_end <= (n_tiles - 1) * tm).astype(I32))
    tile_expert = jnp.where(tile_idx < n_tiles, tile_expert, last_e)
    prev = jnp.concatenate([jnp.full((1,), -1, I32), tile_expert[:-1]])
    tile_first = (tile_expert != prev).astype(I32)
    tile_parity = (jnp.cumsum(tile_first) - 1) & 1
    experts = jnp.arange(N_EXPERTS, dtype=I32)
    later = jnp.where((counts[None, :] > 0) & (experts[None, :] > experts[:, None]), experts[None, :], N_EXPERTS)
    nxt = jnp.min(later, axis=1)
    nxt = jnp.where(nxt < N_EXPERTS, nxt, -1)
    next_expert = nxt[tile_expert]
    tile_rows = jnp.clip(counts[tile_expert] - (tile_idx * tm - pad_off[tile_expert]), 0, tm)
    tables = (tile_expert, tile_first, next_expert, tile_parity.astype(I32), tile_rows.astype(I32),
              n_tiles.reshape(1).astype(I32))
    return tables, pad_off.astype(I32)


def kernel(x, p, norm_mix, w_in, conv3_w, sgu_ln_g, sgu_ln_b, sgu_w, sgu_b, cfm_conv_w, cfm_conv_b, cfm_ln_g,
           cfm_ln_b, w_out, norm_moe, router_group_w, router_group_b, router_expert_w, router_expert_b,
           exp_w1, exp_w3, exp_w2, norm_ple, ple_gate_w, ple_proj_w, final_norm):
    bsz, seq, d = x.shape
    depth = w_in.shape[0]
    t = bsz * seq
    assert bsz == 1, "conv history is carried across row tiles of a single sequence"
    moe_tm = min(256, t)
    h = x.reshape(t, d)
    hb = h.astype(BF16)
    row = lambda a: a.reshape(depth, 1, -1)
    tril = jnp.tril(jnp.ones((CHUNK, CHUNK), dtype=bool))
    mixer_params = (
        conv3_w, row(sgu_ln_g), row(sgu_ln_b),
        jnp.where(tril[None, None], sgu_w, 0.0).astype(BF16),
        jnp.repeat(jnp.swapaxes(sgu_b, 1, 2), HEAD_DIM, axis=2),
        jnp.repeat(cfm_conv_w, SUBLANES, axis=1),
        row(cfm_conv_b), row(cfm_ln_g), row(cfm_ln_b))
    pad_lanes = ROUTER_LANES - N_GROUPS - N_EXPERTS
    wr = jnp.concatenate([router_group_w, router_expert_w, jnp.zeros((depth, d, pad_lanes), F32)], axis=2)
    br = jnp.concatenate([router_group_b, router_expert_b, jnp.zeros((depth, pad_lanes), F32)], axis=1)
    g_mix, g_moe, g_ple = norm_mix.reshape(depth, d, 1), row(norm_moe), row(norm_ple)
    p3 = p.reshape(depth, t, -1)
    for i in range(depth):
        proj = in_proj(hb, w_in, g_mix, i)
        y = mixers(proj, i, mixer_params)
        h = out_proj(y, w_out, h, i)
        route, xs, cnt = router(h, g_moe, wr, row(br), i)
        code = route[:, RL_CODE1:RL_CODE2 + 1].astype(I32).reshape(-1)
        counts = cnt[0, N_GROUPS:N_GROUPS + N_EXPERTS].astype(I32)
        tables, pad_off = _dispatch_tables(counts, 2 * t, moe_tm)
        pos = pad_off[code & (N_EXPERTS - 1)] + lax.shift_right_logical(code, EXPERT_BITS)
        ys = moe_experts(xs, tables, pos, exp_w1, exp_w3, exp_w2, i, moe_tm)
        h, hn = moe_combine(h, g_ple, ys, pos, i)
        h, hb = ple_update(hn, ple_gate_w, p3, ple_proj_w, h, i)
    return rmsnorm(h, final_norm, F32).reshape(bsz, seq, d)
```

```python
import functools

import jax
import jax.numpy as jnp
from jax import lax
from jax.experimental import pallas as pl
from jax.experimental.pallas import tpu as pltpu

F32 = jnp.float32
BF16 = jnp.bfloat16
U32 = jnp.uint32
I32 = jnp.int32
EPS = 1e-6

LANES = 128
SUBLANES = 8
HEAD_DIM = 128
CHUNK = 128
N_B_HEADS = 8
A_W = 12 * HEAD_DIM
B_W = N_B_HEADS * HEAD_DIM
C_W = 12 * HEAD_DIM
MIX_W = A_W + B_W + C_W
OFF_AB, OFF_AC, OFF_AX = 0, A_W, 2 * A_W
OFF_BU, OFF_BV = 3 * A_W, 3 * A_W + B_W
OFF_CA, OFF_CG = 3 * A_W + 2 * B_W, 3 * A_W + 2 * B_W + C_W
IN_COLS = 3 * A_W + 2 * B_W + 2 * C_W
CFM_K = 31
CFM_HALO = 32
CONV_STRIDE = 4
CONV_ROWS = SUBLANES * CONV_STRIDE
N_GROUPS = 4
EXPERTS_PER_GROUP = 8
N_EXPERTS = N_GROUPS * EXPERTS_PER_GROUP
EXPERT_BITS = 5
assert N_EXPERTS == 1 << EXPERT_BITS
ROUTER_LANES = LANES
RL_W1, RL_W2, RL_CODE1, RL_CODE2, RL_E1, RL_E2 = 0, 1, 2, 3, 4, 5
VMEM_LIMIT = 56 * 1024 * 1024


def _params(n_axes=1):
    return pltpu.CompilerParams(dimension_semantics=("arbitrary",) * n_axes,
                                vmem_limit_bytes=VMEM_LIMIT)


def _slab_dims(d):
    rows = d // 2 // LANES
    return rows, rows + SUBLANES


def _pack_pair(hi, lo):
    hi = lax.bitcast_convert_type(hi.astype(BF16).astype(F32), U32)
    lo = lax.bitcast_convert_type(lo.astype(BF16).astype(F32), U32)
    return hi | (lo >> 16)


def _pack_rows(x):
    half = x.shape[1] // 2
    return _pack_pair(x[:, :half], x[:, half:])


def _unpack_rows(w):
    hi = lax.bitcast_convert_type(w & jnp.uint32(0xFFFF0000), F32)
    lo = lax.bitcast_convert_type(w << 16, F32)
    return hi, lo


def _store_slab_rows(ref, w, m, pitch, row0):
    for s in range(w.shape[1] // LANES):
        ref[pl.ds(row0 + s, m, stride=pitch), :] = w[:, s * LANES:(s + 1) * LANES]


def _zero_slab_rows(ref, m, pitch, first, last):
    for s in range(first, last):
        ref[pl.ds(s, m, stride=pitch), :] = jnp.zeros((m, LANES), ref.dtype)


def _store_slabs(ref, w, m, rows, pitch, meta=None):
    _store_slab_rows(ref, w, m, pitch, 0)
    first_pad = rows
    if meta is not None:
        ref[pl.ds(rows, m, stride=pitch), :] = meta
        first_pad = rows + 1
    _zero_slab_rows(ref, m, pitch, first_pad, pitch)


def _load_slabs(ref, lead, row0, m, rows, pitch):
    return jnp.concatenate([ref[lead, pl.ds(row0 + s, m, stride=pitch), :] for s in range(rows)], axis=1)


def _rms_kernel(x_ref, g_ref, o_ref):
    x = x_ref[...]
    ms = jnp.mean(x * x, axis=-1, keepdims=True)
    o_ref[...] = (x * lax.rsqrt(ms + EPS) * g_ref[...]).astype(o_ref.dtype)


def rmsnorm(x, g, out_dtype, tm=256):
    t, d = x.shape
    tm = min(tm, t)
    return pl.pallas_call(
        _rms_kernel, grid=(t // tm,),
        in_specs=[pl.BlockSpec((tm, d), lambda i: (i, 0)), pl.BlockSpec((1, d), lambda i: (0, 0))],
        out_specs=pl.BlockSpec((tm, d), lambda i: (i, 0)),
        out_shape=jax.ShapeDtypeStruct((t, d), out_dtype),
        compiler_params=_params(), name="rmsnorm")(x, g.reshape(1, d))


def _in_proj_kernel(x_ref, w_ref, g_ref, o_ref, wb, rs):
    j, i = pl.program_id(0), pl.program_id(1)

    @pl.when(i == 0)
    def _():
        wb[...] = (w_ref[...] * g_ref[...]).astype(BF16)

    @pl.when(j == 0)
    def _():
        xf = x_ref[...].astype(F32)
        rs[i] = lax.rsqrt(jnp.mean(xf * xf, axis=-1, keepdims=True) + EPS)

    acc = jnp.dot(x_ref[...], wb[...], preferred_element_type=F32)
    o_ref[...] = (acc * rs[i]).astype(o_ref.dtype)


def in_proj(hb, w, g, layer, tm=1024, tn=512):
    t, k = hb.shape
    n = w.shape[2]
    tm, tn = min(tm, t), min(tn, n)
    return pl.pallas_call(
        _in_proj_kernel, grid=(n // tn, t // tm),
        in_specs=[pl.BlockSpec((tm, k), lambda j, i: (i, 0)), pl.BlockSpec((None, k, tn), lambda j, i: (layer, 0, j)),
                  pl.BlockSpec((None, k, 1), lambda j, i: (layer, 0, 0))],
        out_specs=pl.BlockSpec((tm, tn), lambda j, i: (i, j)),
        out_shape=jax.ShapeDtypeStruct((t, n), BF16),
        scratch_shapes=[pltpu.VMEM((k, tn), BF16), pltpu.VMEM((t // tm, tm, 1), F32)],
        compiler_params=_params(2), name="in_proj")(hb, w, g)


def _out_proj_kernel(y_ref, w_ref, r_ref, o_ref, wb):
    @pl.when(pl.program_id(1) == 0)
    def _():
        wb[...] = w_ref[...].astype(BF16)

    o_ref[...] = r_ref[...] + jnp.dot(y_ref[...], wb[...], preferred_element_type=F32)


def out_proj(y, w, res, layer, tm=1024, tn=512):
    t, k = y.shape
    n = w.shape[2]
    tm, tn = min(tm, t), min(tn, n)
    return pl.pallas_call(
        _out_proj_kernel, grid=(n // tn, t // tm),
        in_specs=[pl.BlockSpec((tm, k), lambda j, i: (i, 0)), pl.BlockSpec((None, k, tn), lambda j, i: (layer, 0, j)),
                  pl.BlockSpec((tm, tn), lambda j, i: (i, j))],
        out_specs=pl.BlockSpec((tm, tn), lambda j, i: (i, j)),
        out_shape=jax.ShapeDtypeStruct((t, n), F32),
        scratch_shapes=[pltpu.VMEM((k, tn), BF16)],
        compiler_params=_params(2), name="out_proj")(y, w, res)


def _ple_kernel(x_ref, wg_ref, p_ref, wp_ref, r_ref, o_ref, ob_ref, wgb, wpb):
    @pl.when(pl.program_id(1) == 0)
    def _():
        wgb[...] = wg_ref[...].astype(BF16)
        wpb[...] = wp_ref[...].astype(BF16)

    gate = jax.nn.sigmoid(jnp.dot(x_ref[...], wgb[...], preferred_element_type=F32))
    pp = jnp.dot(p_ref[...].astype(BF16), wpb[...], preferred_element_type=F32)
    h = r_ref[...] + gate * pp
    o_ref[...] = h
    ob_ref[...] = h.astype(BF16)


def ple_update(xn, wg, p, wp, res, layer, tm=1024, tn=512):
    t, k = xn.shape
    n = wg.shape[2]
    r = p.shape[2]
    tm, tn = min(tm, t), min(tn, n)
    return pl.pallas_call(
        _ple_kernel, grid=(n // tn, t // tm),
        in_specs=[pl.BlockSpec((tm, k), lambda j, i: (i, 0)), pl.BlockSpec((None, k, tn), lambda j, i: (layer, 0, j)),
                  pl.BlockSpec((None, tm, r), lambda j, i: (layer, i, 0)),
                  pl.BlockSpec((None, r, tn), lambda j, i: (layer, 0, j)),
                  pl.BlockSpec((tm, tn), lambda j, i: (i, j))],
        out_specs=[pl.BlockSpec((tm, tn), lambda j, i: (i, j)), pl.BlockSpec((tm, tn), lambda j, i: (i, j))],
        out_shape=[jax.ShapeDtypeStruct((t, n), F32), jax.ShapeDtypeStruct((t, n), BF16)],
        scratch_shapes=[pltpu.VMEM((k, tn), BF16), pltpu.VMEM((r, tn), BF16)],
        compiler_params=_params(2), name="ple_update")(xn, wg, p, wp, res)


def _layer_norm(x, g, b):
    mu = jnp.mean(x, axis=-1, keepdims=True)
    xc = x - mu
    var = jnp.mean(xc * xc, axis=-1, keepdims=True)
    return xc * lax.rsqrt(var + EPS) * g + b


def _mixer_kernel(proj_ref, c3w_ref, lng_ref, lnb_ref, ws_ref, bexp_ref, cwb_ref, cb_ref, clg_ref, clb_ref,
                  y_ref, cx_ext, g_s, conv_s):
    tm = y_ref.shape[0]
    n_slab = C_W // LANES

    @pl.when(pl.program_id(0) == 0)
    def _():
        cx_ext[0:SUBLANES, :] = jnp.zeros((SUBLANES, A_W), F32)
        g_s[:, 0:CFM_HALO, :] = jnp.zeros((n_slab, CFM_HALO, LANES), F32)

    a_c = proj_ref[:, OFF_AC:OFF_AC + A_W].astype(F32)
    a_x = proj_ref[:, OFF_AX:OFF_AX + A_W].astype(F32)
    cx_ext[SUBLANES:SUBLANES + tm, :] = a_c * a_x
    conv = c3w_ref[0:1, :] * cx_ext[SUBLANES - 2:SUBLANES - 2 + tm, :]
    conv += c3w_ref[1:2, :] * cx_ext[SUBLANES - 1:SUBLANES - 1 + tm, :]
    conv += c3w_ref[2:3, :] * cx_ext[SUBLANES:SUBLANES + tm, :]
    a_b = proj_ref[:, OFF_AB:OFF_AB + A_W].astype(F32)
    y_ref[:, 0:A_W] = (a_b * conv).astype(y_ref.dtype)
    cx_ext[0:SUBLANES, :] = cx_ext[tm:tm + SUBLANES, :]

    for r0 in range(0, tm, CHUNK):
        u = proj_ref[r0:r0 + CHUNK, OFF_BU:OFF_BU + B_W].astype(F32)
        v = proj_ref[r0:r0 + CHUNK, OFF_BV:OFF_BV + B_W].astype(F32)
        zu = jax.nn.gelu(u)
        zv = _layer_norm(jax.nn.gelu(v), lng_ref[...], lnb_ref[...]).astype(BF16)
        for h in range(N_B_HEADS):
            c0 = h * HEAD_DIM
            s = jnp.dot(ws_ref[h], zv[:, c0:c0 + HEAD_DIM], preferred_element_type=F32)
            s = s + bexp_ref[:, c0:c0 + HEAD_DIM]
            y_ref[r0:r0 + CHUNK, A_W + c0:A_W + c0 + HEAD_DIM] = (zu[:, c0:c0 + HEAD_DIM] * s).astype(y_ref.dtype)

    c_a = proj_ref[:, OFF_CA:OFF_CA + C_W].astype(F32)
    c_g = proj_ref[:, OFF_CG:OFF_CG + C_W].astype(F32)
    glu = c_a * jax.nn.sigmoid(c_g)
    for c in range(n_slab):
        g_s[c, CFM_HALO:CFM_HALO + tm, :] = glu[:, c * LANES:(c + 1) * LANES]
    first_tap_row = CFM_HALO - (CFM_K - 1)

    def conv_rows(rb, carry):
        base = pl.multiple_of(rb * CONV_ROWS, CONV_ROWS)
        for c in range(n_slab):
            accs = [jnp.zeros((SUBLANES, LANES), F32) for _ in range(CONV_STRIDE)]
            for k in range(CFM_K):
                wv = cwb_ref[k * SUBLANES:(k + 1) * SUBLANES, c * LANES:(c + 1) * LANES]
                for j in range(CONV_STRIDE):
                    rows = pl.ds(base + (first_tap_row + k + j), SUBLANES, stride=CONV_STRIDE)
                    accs[j] = accs[j] + wv * g_s[c, rows, :]
            for j in range(CONV_STRIDE):
                conv_s[c, pl.ds(base + j, SUBLANES, stride=CONV_STRIDE), :] = accs[j]
        blk = jnp.concatenate([conv_s[c, pl.ds(base, CONV_ROWS), :] for c in range(n_slab)], axis=1) + cb_ref[...]
        z = _layer_norm(blk, clg_ref[...], clb_ref[...])
        y_ref[pl.ds(base, CONV_ROWS), A_W + B_W:MIX_W] = jax.nn.silu(z).astype(y_ref.dtype)
        return carry

    lax.fori_loop(0, tm // CONV_ROWS, conv_rows, 0)
    g_s[:, 0:CFM_HALO, :] = g_s[:, tm:tm + CFM_HALO, :]


def mixers(proj, layer, stacked, tm=256):
    t = proj.shape[0]
    tm = min(tm, t)
    per_layer = lambda a: pl.BlockSpec((None,) + a.shape[1:], lambda i: (layer,) + (0,) * (a.ndim - 1))
    n_slab = C_W // LANES
    return pl.pallas_call(
        _mixer_kernel, grid=(t // tm,),
        in_specs=[pl.BlockSpec((tm, IN_COLS), lambda i: (i, 0))] + [per_layer(a) for a in stacked],
        out_specs=pl.BlockSpec((tm, MIX_W), lambda i: (i, 0)),
        out_shape=jax.ShapeDtypeStruct((t, MIX_W), BF16),
        scratch_shapes=[pltpu.VMEM((SUBLANES + tm, A_W), F32), pltpu.VMEM((n_slab, CFM_HALO + tm, LANES), F32),
                        pltpu.VMEM((n_slab, tm, LANES), F32)],
        compiler_params=_params(), name="mixers")(proj, *stacked)


def _router_kernel(h_ref, g_ref, w_ref, b_ref, o_ref, xs_ref, cnt_ref, run):
    tm, d = h_ref.shape

    @pl.when(pl.program_id(0) == 0)
    def _():
        run[...] = jnp.zeros(run.shape, F32)

    x = h_ref[...]
    ms = jnp.mean(x * x, axis=-1, keepdims=True)
    hn = x * lax.rsqrt(ms + EPS) * g_ref[...]
    logits = jnp.dot(hn.astype(BF16), w_ref[...].astype(BF16), preferred_element_type=F32) + b_ref[...]
    lane = lax.broadcasted_iota(I32, logits.shape, 1)
    neg = jnp.float32(-jnp.inf)
    is_g = lane < N_GROUPS
    gl = jnp.where(is_g, logits, neg)
    gmax = jnp.max(gl, axis=-1, keepdims=True)
    gidx = jnp.min(jnp.where(gl == gmax, lane, ROUTER_LANES), axis=-1, keepdims=True)
    gsum = jnp.sum(jnp.where(is_g, jnp.exp(gl - gmax), 0.0), axis=-1, keepdims=True)
    g_p = 1.0 / gsum
    e_lane = lane - N_GROUPS
    in_grp = (e_lane >= 0) & (e_lane < N_EXPERTS) & (lax.shift_right_arithmetic(e_lane, 3) == gidx)
    el = jnp.where(in_grp, logits, neg)
    v1 = jnp.max(el, axis=-1, keepdims=True)
    i1 = jnp.min(jnp.where(el == v1, lane, ROUTER_LANES), axis=-1, keepdims=True)
    el2 = jnp.where(lane == i1, neg, el)
    v2 = jnp.max(el2, axis=-1, keepdims=True)
    i2 = jnp.min(jnp.where(el2 == v2, lane, ROUTER_LANES), axis=-1, keepdims=True)
    t2 = jnp.exp(v2 - v1)
    w1 = g_p / (1.0 + t2)
    w2 = w1 * t2
    oh1 = jnp.where(lane == i1, 1.0, 0.0)
    oh2 = jnp.where(lane == i2, 1.0, 0.0)
    r_i = lax.broadcasted_iota(I32, (tm, tm), 0)
    c_i = lax.broadcasted_iota(I32, (tm, tm), 1)
    ltri = jnp.where(r_i > c_i, 1.0, 0.0).astype(BF16)
    before1 = jnp.dot(ltri, oh1.astype(BF16), preferred_element_type=F32)
    before2 = jnp.dot(ltri, oh2.astype(BF16), preferred_element_type=F32)
    tot1 = jnp.sum(oh1, axis=0, keepdims=True)
    tot2 = jnp.sum(oh2, axis=0, keepdims=True)
    seen = run[0:1, :]
    rank1 = jnp.sum(oh1 * (seen + before1), axis=-1, keepdims=True)
    rank2 = jnp.sum(oh2 * (seen + tot1 + before2), axis=-1, keepdims=True)
    seen = seen + tot1 + tot2
    run[...] = jnp.broadcast_to(seen, run.shape)
    cnt_ref[...] = jnp.broadcast_to(seen, cnt_ref.shape)
    e1 = (i1 - N_GROUPS).astype(F32)
    e2 = (i2 - N_GROUPS).astype(F32)
    out = jnp.where(lane == RL_W1, w1, 0.0)
    out = jnp.where(lane == RL_W2, w2, out)
    out = jnp.where(lane == RL_CODE1, rank1 * N_EXPERTS + e1, out)
    out = jnp.where(lane == RL_CODE2, rank2 * N_EXPERTS + e2, out)
    out = jnp.where(lane == RL_E1, e1, out)
    out = jnp.where(lane == RL_E2, e2, out)
    o_ref[...] = out
    rows, pitch = _slab_dims(d)
    _store_slabs(xs_ref, _pack_rows(hn), tm, rows, pitch, meta=lax.bitcast_convert_type(out, U32))


def router(h, g, w, b, layer, tm=256):
    t, d = h.shape
    tm = min(tm, t)
    _, pitch = _slab_dims(d)
    return pl.pallas_call(
        _router_kernel, grid=(t // tm,),
        in_specs=[pl.BlockSpec((tm, d), lambda i: (i, 0)), pl.BlockSpec((None, 1, d), lambda i: (layer, 0, 0)),
                  pl.BlockSpec((None, d, ROUTER_LANES), lambda i: (layer, 0, 0)),
                  pl.BlockSpec((None, 1, ROUTER_LANES), lambda i: (layer, 0, 0))],
        out_specs=[pl.BlockSpec((tm, ROUTER_LANES), lambda i: (i, 0)),
                   pl.BlockSpec((tm * pitch, LANES), lambda i: (i, 0)),
                   pl.BlockSpec((SUBLANES, ROUTER_LANES), lambda i: (0, 0))],
        out_shape=[jax.ShapeDtypeStruct((t, ROUTER_LANES), F32),
                   jax.ShapeDtypeStruct((t * pitch, LANES), U32),
                   jax.ShapeDtypeStruct((SUBLANES, ROUTER_LANES), F32)],
        scratch_shapes=[pltpu.VMEM((SUBLANES, ROUTER_LANES), F32)],
        compiler_params=_params(), name="router")(h, g, w, b)


def _slab_copy(src_hbm, dst_vmem, src_slab, dst_slab, rows, pitch, sem):
    return pltpu.make_async_copy(src_hbm.at[pl.ds(src_slab * pitch, rows), :],
                                 dst_vmem.at[pl.ds(dst_slab * pitch, rows), :], sem)


def _moe_kernel(te_ref, tf_ref, nx_ref, par_ref, tr_ref, nt_ref, pos_ref,
                xs_hbm, w1_hbm, w3_hbm, w2_hbm, o_ref,
                row_slab, xbuf, ws1, ws3, ws2, w1b, w3b, w2b, xsems, wsems, *, layer):
    d = ws1.shape[1]
    rows, pitch = _slab_dims(d)
    tm = xbuf.shape[1] // pitch
    i = pl.program_id(0)
    n_used = nt_ref[0]
    slot = lax.rem(i, 2)

    def start_gather(tile, dst_slot):
        base = tile * tm

        def body(r, c):
            _slab_copy(xs_hbm, xbuf.at[dst_slot], row_slab[base + r], r, pitch, pitch,
                       xsems.at[dst_slot]).start()
            return c

        lax.fori_loop(0, tm, body, 0, unroll=8)

    def wait_gather(dst_slot):
        n = tm * pitch
        pltpu.make_async_copy(xs_hbm.at[pl.ds(0, n), :], xbuf.at[dst_slot].at[pl.ds(0, n), :],
                              xsems.at[dst_slot]).wait()

    def weights(expert, wslot, start):
        for src, dst in ((w1_hbm, ws1), (w3_hbm, ws3), (w2_hbm, ws2)):
            cp = pltpu.make_async_copy(src.at[layer, expert], dst.at[wslot], wsems.at[wslot])
            cp.start() if start else cp.wait()

    @pl.when(i == 0)
    def _():
        def clear(r, c):
            row_slab[r] = 0
            return c

        lax.fori_loop(0, row_slab.shape[0], clear, 0, unroll=16)

        def place(q, c):
            row_slab[pos_ref[q]] = lax.shift_right_logical(q, 1)
            return c

        lax.fori_loop(0, pos_ref.shape[0], place, 0, unroll=16)
        weights(te_ref[0], 0, True)
        start_gather(0, 0)

    @pl.when(i < n_used)
    def _():
        expert = te_ref[i]
        wait_gather(slot)

        @pl.when(i + 1 < n_used)
        def _():
            start_gather(i + 1, 1 - slot)

        @pl.when(tf_ref[i] == 1)
        def _():
            wslot = par_ref[i]
            weights(expert, wslot, False)

            @pl.when(nx_ref[i] >= 0)
            def _():
                weights(nx_ref[i], 1 - wslot, True)

            w1b[...] = ws1[wslot].astype(BF16)
            w3b[...] = ws3[wslot].astype(BF16)
            w2b[...] = ws2[wslot].astype(BF16)

        hi, lo = _unpack_rows(_load_slabs(xbuf, slot, 0, tm, rows, pitch))
        hn = jnp.concatenate([hi.astype(BF16), lo.astype(BF16)], axis=1)
        meta = lax.bitcast_convert_type(xbuf[slot, pl.ds(rows, tm, stride=pitch), :], F32)
        gate = jnp.where(meta[:, RL_E1:RL_E1 + 1] == expert.astype(F32),
                         meta[:, RL_W1:RL_W1 + 1], meta[:, RL_W2:RL_W2 + 1])
        gate = jnp.where(lax.broadcasted_iota(I32, gate.shape, 0) < tr_ref[i], gate, 0.0)
        h1 = jnp.dot(hn, w1b[...], preferred_element_type=F32)
        h3 = jnp.dot(hn, w3b[...], preferred_element_type=F32)
        hid = (jax.nn.silu(h1) * h3 * gate).astype(BF16)
        out = jnp.dot(hid, w2b[...], preferred_element_type=F32)
        _store_slabs(o_ref, _pack_rows(out), tm, rows, pitch)

    @pl.when(i >= n_used)
    def _():
        o_ref[...] = jnp.zeros(o_ref.shape, o_ref.dtype)


def moe_experts(xs, tables, pos, w1, w3, w2, layer, tm):
    tile_expert, tile_first, next_expert, tile_parity, tile_rows, n_tiles = tables
    nt = tile_expert.shape[0]
    d, ff = w1.shape[2], w1.shape[3]
    _, pitch = _slab_dims(d)
    any_spec = pl.BlockSpec(memory_space=pl.ANY)
    grid_spec = pltpu.PrefetchScalarGridSpec(
        num_scalar_prefetch=7, grid=(nt,),
        in_specs=[any_spec, any_spec, any_spec, any_spec],
        out_specs=pl.BlockSpec((tm * pitch, LANES), lambda i, *_: (i, 0)),
        scratch_shapes=[pltpu.SMEM((nt * tm,), I32),
                        pltpu.VMEM((2, tm * pitch, LANES), U32),
                        pltpu.VMEM((2, d, ff), F32), pltpu.VMEM((2, d, ff), F32), pltpu.VMEM((2, ff, d), F32),
                        pltpu.VMEM((d, ff), BF16), pltpu.VMEM((d, ff), BF16), pltpu.VMEM((ff, d), BF16),
                        pltpu.SemaphoreType.DMA((2,)), pltpu.SemaphoreType.DMA((2,))])
    return pl.pallas_call(
        functools.partial(_moe_kernel, layer=layer), grid_spec=grid_spec,
        out_shape=jax.ShapeDtypeStruct((nt * tm * pitch, LANES), U32),
        compiler_params=_params(), name="moe_experts")(
            tile_expert, tile_first, next_expert, tile_parity, tile_rows, n_tiles, pos,
            xs, w1, w3, w2)


def _combine_kernel(pos_ref, h_ref, g_ref, ys_hbm, o_ref, on_ref, buf, sems):
    tm, d = h_ref.shape
    rows, pitch = _slab_dims(d)
    i = pl.program_id(0)
    slot = lax.rem(i, 2)

    def start_gather(tile, dst_slot):
        base = tile * tm

        def body(r, c):
            for k in range(2):
                _slab_copy(ys_hbm, buf.at[dst_slot], pos_ref[2 * (base + r) + k], k * tm + r, rows, pitch,
                           sems.at[dst_slot]).start()
            return c

        lax.fori_loop(0, tm, body, 0, unroll=4)

    def wait_gather(dst_slot):
        n = 2 * tm * rows
        pltpu.make_async_copy(ys_hbm.at[pl.ds(0, n), :], buf.at[dst_slot].at[pl.ds(0, n), :],
                              sems.at[dst_slot]).wait()

    @pl.when(i == 0)
    def _():
        start_gather(0, 0)

    wait_gather(slot)

    @pl.when(i + 1 < pl.num_programs(0))
    def _():
        start_gather(i + 1, 1 - slot)

    h = h_ref[...]
    for k in range(2):
        hi, lo = _unpack_rows(_load_slabs(buf, slot, k * tm * pitch, tm, rows, pitch))
        h = h + jnp.concatenate([hi, lo], axis=1)
    o_ref[...] = h
    ms = jnp.mean(h * h, axis=-1, keepdims=True)
    on_ref[...] = (h * lax.rsqrt(ms + EPS) * g_ref[...]).astype(on_ref.dtype)


def moe_combine(h, g, ys, pos, layer, tm=256):
    t, d = h.shape
    tm = min(tm, t)
    _, pitch = _slab_dims(d)
    grid_spec = pltpu.PrefetchScalarGridSpec(
        num_scalar_prefetch=1, grid=(t // tm,),
        in_specs=[pl.BlockSpec((tm, d), lambda i, *_: (i, 0)),
                  pl.BlockSpec((None, 1, d), lambda i, *_: (layer, 0, 0)),
                  pl.BlockSpec(memory_space=pl.ANY)],
        out_specs=[pl.BlockSpec((tm, d), lambda i, *_: (i, 0)), pl.BlockSpec((tm, d), lambda i, *_: (i, 0))],
        scratch_shapes=[pltpu.VMEM((2, 2 * tm * pitch, LANES), U32), pltpu.SemaphoreType.DMA((2,))])
    return pl.pallas_call(
        _combine_kernel, grid_spec=grid_spec,
        out_shape=[jax.ShapeDtypeStruct((t, d), F32), jax.ShapeDtypeStruct((t, d), BF16)],
        compiler_params=_params(), name="moe_combine")(pos, h, g, ys)


def _dispatch_tables(counts, n_pairs, tm):
    nt = n_pairs // tm + N_EXPERTS
    padded = ((counts + tm - 1) // tm) * tm
    pad_end = jnp.cumsum(padded)
    pad_off = pad_end - padded
    n_tiles = pad_end[-1] // tm
    tile_idx = jnp.arange(nt, dtype=I32)
    tile_expert = jnp.sum((pad_end[None, :] <= (tile_idx * tm)[:, None]).astype(I32), axis=1)
    last_e = jnp.sum((pad_end <= (n_tiles - 1) * tm).astype(I32))
    tile_expert = jnp.where(tile_idx < n_tiles, tile_expert, last_e)
    prev = jnp.concatenate([jnp.full((1,), -1, I32), tile_expert[:-1]])
    tile_first = (tile_expert != prev).astype(I32)
    tile_parity = (jnp.cumsum(tile_first) - 1) & 1
    experts = jnp.arange(N_EXPERTS, dtype=I32)
    later = jnp.where((counts[None, :] > 0) & (experts[None, :] > experts[:, None]), experts[None, :], N_EXPERTS)
    nxt = jnp.min(later, axis=1)
    nxt = jnp.where(nxt < N_EXPERTS, nxt, -1)
    next_expert = nxt[tile_expert]
    tile_rows = jnp.clip(counts[tile_expert] - (tile_idx * tm - pad_off[tile_expert]), 0, tm)
    tables = (tile_expert, tile_first, next_expert, tile_parity.astype(I32), tile_rows.astype(I32),
              n_tiles.reshape(1).astype(I32))
    return tables, pad_off.astype(I32)


def kernel(x, p, norm_mix, w_in, conv3_w, sgu_ln_g, sgu_ln_b, sgu_w, sgu_b, cfm_conv_w, cfm_conv_b, cfm_ln_g,
           cfm_ln_b, w_out, norm_moe, router_group_w, router_group_b, router_expert_w, router_expert_b,
           exp_w1, exp_w3, exp_w2, norm_ple, ple_gate_w, ple_proj_w, final_norm):
    bsz, seq, d = x.shape
    depth = w_in.shape[0]
    t = bsz * seq
    assert bsz == 1, "conv history is carried across row tiles of a single sequence"
    moe_tm = min(256, t)
    h = x.reshape(t, d)
    hb = h.astype(BF16)
    row = lambda a: a.reshape(depth, 1, -1)
    tril = jnp.tril(jnp.ones((CHUNK, CHUNK), dtype=bool))
    mixer_params = (
        conv3_w, row(sgu_ln_g), row(sgu_ln_b),
        jnp.where(tril[None, None], sgu_w, 0.0).astype(BF16),
        jnp.repeat(jnp.swapaxes(sgu_b, 1, 2), HEAD_DIM, axis=2),
        jnp.repeat(cfm_conv_w, SUBLANES, axis=1),
        row(cfm_conv_b), row(cfm_ln_g), row(cfm_ln_b))
    pad_lanes = ROUTER_LANES - N_GROUPS - N_EXPERTS
    wr = jnp.concatenate([router_group_w, router_expert_w, jnp.zeros((depth, d, pad_lanes), F32)], axis=2)
    br = jnp.concatenate([router_group_b, router_expert_b, jnp.zeros((depth, pad_lanes), F32)], axis=1)
    g_mix, g_moe, g_ple = norm_mix.reshape(depth, d, 1), row(norm_moe), row(norm_ple)
    p3 = p.reshape(depth, t, -1)
    for i in range(depth):
        proj = in_proj(hb, w_in, g_mix, i)
        y = mixers(proj, i, mixer_params)
        h = out_proj(y, w_out, h, i)
        route, xs, cnt = router(h, g_moe, wr, row(br), i)
        code = route[:, RL_CODE1:RL_CODE2 + 1].astype(I32).reshape(-1)
        counts = cnt[0, N_GROUPS:N_GROUPS + N_EXPERTS].astype(I32)
        tables, pad_off = _dispatch_tables(counts, 2 * t, moe_tm)
        pos = pad_off[code & (N_EXPERTS - 1)] + lax.shift_right_logical(code, EXPERT_BITS)
        ys = moe_experts(xs, tables, pos, exp_w1, exp_w3, exp_w2, i, moe_tm)
        h, hn = moe_combine(h, g_ple, ys, pos, i)
        h, hb = ple_update(hn, ple_gate_w, p3, ple_proj_w, h, i)
    return rmsnorm(h, final_norm, F32).reshape(bsz, seq, d)
```
